```python
import math
import jax, jax.numpy as jnp
from jax import lax
import numpy as np

D_MODEL = 1024
BATCH = 2
SEQ = 8192
DEPTH = 2
DEC_BATCH = 32
DEC_SEQ = 1
PAST_LEN = 8192
PAGE_SIZE = 128

N_META = 16
BLOCK = 128
HEAD_DIM = 64
SSD_WIDTH = 3 * D_MODEL // 8
SSD_HEADS = SSD_WIDTH // HEAD_DIM
SSD_GROUPS = 2
SSD_STATE = 128
SSD_CONV = 4
SSD_XBC = SSD_WIDTH + 2 * SSD_GROUPS * SSD_STATE
SB_WIDTH = 3 * D_MODEL // 8
SB_HEADS = SB_WIDTH // HEAD_DIM
SB_BIAS_INIT = -6.0
RG_WIDTH = D_MODEL // 4
RG_BLOCKS = 4
RG_BLOCK_DIM = RG_WIDTH // RG_BLOCKS
RG_CONV = 4
RG_C = 8.0
MIX_WIDTH = SSD_WIDTH + SB_WIDTH + RG_WIDTH
IN_COLS = 2 * SSD_WIDTH + 2 * SSD_GROUPS * SSD_STATE + SSD_HEADS + 3 * SB_WIDTH + 2 * RG_WIDTH
D_FF = 2816
FFN_CONV = 3
EPS = 1e-6

kernel_name = 'hymba_ssd_stickbreak_rglru_convffn_step'


def _in_splits():
    sizes = (SSD_WIDTH, SSD_XBC, SSD_HEADS, SB_WIDTH, SB_WIDTH, SB_WIDTH, RG_WIDTH, RG_WIDTH)
    return [int(s) for s in np.cumsum(sizes)[:-1]]


def rms_norm(x, g):
    xf = x.astype(jnp.float32)
    y = xf * lax.rsqrt(jnp.mean(xf * xf, axis=-1, keepdims=True) + EPS)
    return (y * g.astype(jnp.float32)).astype(x.dtype)


def causal_dwconv(x, buf, w, b):
    K = w.shape[0]
    L = x.shape[1]
    xp = jnp.concatenate([buf.astype(x.dtype), x], axis=1)
    y = b + xp[:, 0:L] * w[0]
    for j in range(1, K):
        y = y + xp[:, j:j + L] * w[j]
    return y, xp[:, -(K - 1):]


def ssd_scan(x, dt, A, Bm, Cm, h0, chunk):
    b, L, H, P = x.shape
    nc = L // chunk
    rep = H // Bm.shape[2]

    def to_chunks(t):
        return jnp.moveaxis(t.reshape((b, nc, chunk) + t.shape[2:]), 1, 0)

    tri = jnp.tril(jnp.ones((chunk, chunk), dtype=bool))

    def step(h, inp):
        xc, dtc, Bc, Cc = inp
        Bh = jnp.repeat(Bc, rep, axis=2)
        Ch = jnp.repeat(Cc, rep, axis=2)
        cum = jnp.cumsum(dtc * A, axis=1)
        seg = cum[:, :, None, :] - cum[:, None, :, :]
        decay = jnp.exp(jnp.where(tri[None, :, :, None], seg, -jnp.inf))
        cb = jnp.einsum('bthn,bshn->btsh', Ch, Bh)
        w = cb * decay * dtc[:, None, :, :]
        y_intra = jnp.einsum('btsh,bshp->bthp', w, xc)
        y_inter = jnp.einsum('bthn,bhpn->bthp', Ch, h) * jnp.exp(cum)[..., None]
        dec_end = jnp.exp(cum[:, -1:, :] - cum) * dtc
        h_new = h * jnp.exp(cum[:, -1])[:, :, None, None] + jnp.einsum('bsh,bshp,bshn->bhpn', dec_end, xc, Bh)
        return h_new, y_intra + y_inter

    h_fin, ys = lax.scan(step, h0, (to_chunks(x), to_chunks(dt), to_chunks(Bm), to_chunks(Cm)))
    return jnp.moveaxis(ys, 0, 1).reshape(b, L, H, P), h_fin


def ssd_prompt_scan(x, dt, A, Bm, Cm, h0):
    y0, h = ssd_scan(x[:, :N_META], dt[:, :N_META], A, Bm[:, :N_META], Cm[:, :N_META], h0, N_META)
    y1, h = ssd_scan(x[:, N_META:], dt[:, N_META:], A, Bm[:, N_META:], Cm[:, N_META:], h, BLOCK)
    return jnp.concatenate([y0, y1], axis=1), h


def ssd_sample_scan(x, dt, A, Bm, Cm, h0):
    return ssd_scan(x, dt, A, Bm, Cm, h0, x.shape[1])


def ssd_mixer(z, xbc, dt_raw, conv_buf, h0, p, scan_fn):
    b, L, _ = xbc.shape
    xbc_c, new_buf = causal_dwconv(xbc, conv_buf, p['ssd_conv_w'], p['ssd_conv_b'])
    xbc_c = jax.nn.silu(xbc_c.astype(jnp.float32))
    xs = xbc_c[..., :SSD_WIDTH].reshape(b, L, SSD_HEADS, HEAD_DIM)
    Bm = xbc_c[..., SSD_WIDTH:SSD_WIDTH + SSD_GROUPS * SSD_STATE].reshape(b, L, SSD_GROUPS, SSD_STATE)
    Cm = xbc_c[..., SSD_WIDTH + SSD_GROUPS * SSD_STATE:].reshape(b, L, SSD_GROUPS, SSD_STATE)
    dt = jax.nn.softplus(dt_raw.astype(jnp.float32) + p['ssd_dt_bias'].astype(jnp.float32))
    A = -jnp.exp(p['ssd_a_log'].astype(jnp.float32))
    y, h_new = scan_fn(xs, dt, A, Bm, Cm, h0.astype(jnp.float32))
    y = y + xs * p['ssd_d'].astype(jnp.float32)[:, None]
    y = y.reshape(b, L, SSD_WIDTH) * jax.nn.silu(z.astype(jnp.float32))
    return rms_norm(y, p['ssd_norm']), new_buf, h_new


def stick_breaking(q, k, v, bias, q_pos, k_pos):
    z = jnp.einsum('bqhd,bkhd->bhqk', q, k, preferred_element_type=jnp.float32) * (HEAD_DIM ** -0.5)
    z = z + bias.astype(jnp.float32)[None, :, None, None]
    mask = k_pos[None, :] < q_pos[:, None]
    log_1m = jnp.where(mask, jax.nn.log_sigmoid(-z), 0.0)
    after = lax.cumsum(log_1m, axis=3, reverse=True) - log_1m
    w = jnp.where(mask, jnp.exp(jax.nn.log_sigmoid(z) + after), 0.0)
    return jnp.einsum('bhqk,bkhd->bqhd', w, v.astype(jnp.float32))


def sb_prompt(q, k, v, bias):
    b, T, H, D = q.shape
    pos = jnp.arange(T, dtype=jnp.int32)
    o_meta = stick_breaking(q[:, :N_META], k[:, :N_META], v[:, :N_META], bias, pos[:N_META], pos[:N_META])
    nb = (T - N_META) // BLOCK
    qb = jnp.moveaxis(q[:, N_META:].reshape(b, nb, BLOCK, H, D), 1, 0)
    pb = pos[N_META:].reshape(nb, BLOCK)
    ob = lax.map(lambda a: stick_breaking(a[0], k, v, bias, a[1], pos), (qb, pb))
    ob = jnp.moveaxis(ob, 0, 1).reshape(b, T - N_META, H, D)
    return jnp.concatenate([o_meta, ob], axis=1)


def sb_sample(q, k, v, bias, ck, cv, page_table):
    db, L = q.shape[0], q.shape[1]
    past = page_table.shape[1] * ck.shape[1]
    pk = ck[page_table].reshape(db, past, SB_HEADS, HEAD_DIM)
    pv = cv[page_table].reshape(db, past, SB_HEADS, HEAD_DIM)
    k_all = jnp.concatenate([pk.astype(k.dtype), k], axis=1)
    v_all = jnp.concatenate([pv.astype(v.dtype), v], axis=1)
    q_pos = past + jnp.arange(L, dtype=jnp.int32)
    k_pos = jnp.arange(past + L, dtype=jnp.int32)
    return stick_breaking(q, k_all, v_all, bias, q_pos, k_pos)


def rglru_mixer(xr, gr, conv_buf, h0, p):
    b, L, _ = xr.shape
    xc, new_buf = causal_dwconv(xr, conv_buf, p['rg_conv_w'], p['rg_conv_b'])
    xc = xc.astype(jnp.float32)
    xb = xc.reshape(b, L, RG_BLOCKS, RG_BLOCK_DIM)
    r = jax.nn.sigmoid(jnp.einsum('blhi,hij->blhj', xb, p['rg_wa']) + p['rg_ba']).reshape(b, L, RG_WIDTH)
    i = jax.nn.sigmoid(jnp.einsum('blhi,hij->blhj', xb, p['rg_wx']) + p['rg_bx']).reshape(b, L, RG_WIDTH)
    log_a = RG_C * r * jax.nn.log_sigmoid(p['rg_lambda'].astype(jnp.float32))
    a = jnp.exp(log_a)
    u = jnp.sqrt(-jnp.expm1(2.0 * log_a)) * (i * xc)
    u = u.at[:, 0].add(a[:, 0] * h0.astype(jnp.float32))

    def comb(l, rr):
        return (l[0] * rr[0], rr[0] * l[1] + rr[1])

    _, h = lax.associative_scan(comb, (a, u), axis=1)
    y = h * jax.nn.gelu(gr.astype(jnp.float32))
    return y, new_buf, h[:, -1]


def layer(x, p, ssd_buf, ssd_h, rg_buf, rg_h, ffn_buf, ssd_scan_fn, attend_fn):
    b, L, _ = x.shape
    h = rms_norm(x, p['norm1'])
    proj = jnp.einsum('bld,dc->blc', h, p['w_in'])
    z, xbc, dt_raw, q, k, v, xr, gr = jnp.split(proj, _in_splits(), axis=-1)
    y_ssd, ssd_buf, ssd_h = ssd_mixer(z, xbc, dt_raw, ssd_buf, ssd_h, p, ssd_scan_fn)
    q = rms_norm(q.reshape(b, L, SB_HEADS, HEAD_DIM), p['q_norm'])
    k = rms_norm(k.reshape(b, L, SB_HEADS, HEAD_DIM), p['k_norm'])
    v = v.reshape(b, L, SB_HEADS, HEAD_DIM)
    y_sb = rms_norm(attend_fn(q, k, v, p['sb_bias']).reshape(b, L, SB_WIDTH), p['sb_out_norm'])
    y_rg, rg_buf, rg_h = rglru_mixer(xr, gr, rg_buf, rg_h, p)
    y_rg = rms_norm(y_rg, p['rg_out_norm'])
    mix = jnp.concatenate([y_ssd, y_sb, y_rg], axis=-1).astype(x.dtype)
    x = x + jnp.einsum('blm,md->bld', mix, p['w_out'])
    h = rms_norm(x, p['norm2'])
    g, u = jnp.split(jnp.einsum('bld,df->blf', h, p['w_up']), 2, axis=-1)
    g, ffn_buf = causal_dwconv(g, ffn_buf, p['ffn_conv_w'], p['ffn_conv_b'])
    x = x + jnp.einsum('blf,fd->bld', jax.nn.silu(g) * u, p['w_down'])
    return x, (k, v, ssd_h, ssd_buf, rg_h, rg_buf, ffn_buf)


def setup_inputs(seed: int = 0) -> dict:
    key = jax.random.key(seed)
    ks = iter(jax.random.split(key, 64))

    def nrm(shape, s=1.0):
        return s * jax.random.normal(next(ks), shape, jnp.float32)

    def unif(shape, lo, hi):
        return jax.random.uniform(next(ks), shape, jnp.float32, lo, hi)

    n_pages = PAST_LEN // PAGE_SIZE
    n_used = DEC_BATCH * n_pages
    n_pool = n_used + max(1, n_used // 4)
    perm = jax.random.permutation(next(ks), n_pool).astype(jnp.int32)
    page_table = perm[:n_used].reshape(DEC_BATCH, n_pages)

    dt0 = jnp.exp(unif((DEPTH, SSD_HEADS), math.log(1e-3), math.log(1e-1)))
    a_rg = unif((DEPTH, RG_WIDTH), 0.9, 0.999) ** (1.0 / RG_C)

    return {
        'x_prompt': nrm((BATCH, SEQ, D_MODEL)),
        'x_sample': nrm((DEC_BATCH, DEC_SEQ, D_MODEL)),
        'cache_k': nrm((DEPTH, n_pool, PAGE_SIZE, SB_HEADS, HEAD_DIM)),
        'cache_v': nrm((DEPTH, n_pool, PAGE_SIZE, SB_HEADS, HEAD_DIM)),
        'page_table': page_table,
        'state_ssm': nrm((DEPTH, DEC_BATCH, SSD_HEADS, HEAD_DIM, SSD_STATE), 0.1),
        'state_ssm_conv': nrm((DEPTH, DEC_BATCH, SSD_CONV - 1, SSD_XBC)),
        'state_rg': nrm((DEPTH, DEC_BATCH, RG_WIDTH), 0.5),
        'state_rg_conv': nrm((DEPTH, DEC_BATCH, RG_CONV - 1, RG_WIDTH)),
        'state_ffn_conv': nrm((DEPTH, DEC_BATCH, FFN_CONV - 1, D_FF)),
        'meta_tokens': nrm((N_META, D_MODEL)),
        'norm1': 1.0 + nrm((DEPTH, D_MODEL), 0.02),
        'w_in': nrm((DEPTH, D_MODEL, IN_COLS), D_MODEL ** -0.5),
        'ssd_conv_w': nrm((DEPTH, SSD_CONV, SSD_XBC), SSD_CONV ** -0.5),
        'ssd_conv_b': nrm((DEPTH, SSD_XBC), 0.02),
        'ssd_dt_bias': dt0 + jnp.log(-jnp.expm1(-dt0)),
        'ssd_a_log': jnp.log(unif((DEPTH, SSD_HEADS), 1.0, 16.0)),
        'ssd_d': 1.0 + nrm((DEPTH, SSD_HEADS), 0.02),
        'ssd_norm': 1.0 + nrm((DEPTH, SSD_WIDTH), 0.02),
        'q_norm': 1.0 + nrm((DEPTH, HEAD_DIM), 0.02),
        'k_norm': 1.0 + nrm((DEPTH, HEAD_DIM), 0.02),
        'sb_bias': SB_BIAS_INIT + nrm((DEPTH, SB_HEADS), 0.1),
        'sb_out_norm': 1.0 + nrm((DEPTH, SB_WIDTH), 0.02),
        'rg_conv_w': nrm((DEPTH, RG_CONV, RG_WIDTH), RG_CONV ** -0.5),
        'rg_conv_b': nrm((DEPTH, RG_WIDTH), 0.02),
        'rg_wa': nrm((DEPTH, RG_BLOCKS, RG_BLOCK_DIM, RG_BLOCK_DIM), RG_BLOCK_DIM ** -0.5),
        'rg_ba': nrm((DEPTH, RG_BLOCKS, RG_BLOCK_DIM), 0.02),
        'rg_wx': nrm((DEPTH, RG_BLOCKS, RG_BLOCK_DIM, RG_BLOCK_DIM), RG_BLOCK_DIM ** -0.5),
        'rg_bx': nrm((DEPTH, RG_BLOCKS, RG_BLOCK_DIM), 0.02),
        'rg_lambda': jnp.log(a_rg) - jnp.log1p(-a_rg),
        'rg_out_norm': 1.0 + nrm((DEPTH, RG_WIDTH), 0.02),
        'w_out': nrm((DEPTH, MIX_WIDTH, D_MODEL), MIX_WIDTH ** -0.5),
        'norm2': 1.0 + nrm((DEPTH, D_MODEL), 0.02),
        'w_up': nrm((DEPTH, D_MODEL, 2 * D_FF), D_MODEL ** -0.5),
        'ffn_conv_w': nrm((DEPTH, FFN_CONV, D_FF), FFN_CONV ** -0.5),
        'ffn_conv_b': nrm((DEPTH, D_FF), 0.02),
        'w_down': nrm((DEPTH, D_FF, D_MODEL), D_FF ** -0.5),
    }


def reference(x_prompt, x_sample, cache_k, cache_v, page_table, state_ssm, state_ssm_conv, state_rg,
              state_rg_conv, state_ffn_conv, meta_tokens, norm1, w_in, ssd_conv_w, ssd_conv_b, ssd_dt_bias,
              ssd_a_log, ssd_d, ssd_norm, q_norm, k_norm, sb_bias, sb_out_norm, rg_conv_w, rg_conv_b, rg_wa, rg_ba,
              rg_wx, rg_bx, rg_lambda, rg_out_norm, w_out, norm2, w_up, ffn_conv_w, ffn_conv_b, w_down):
    bp = x_prompt.shape[0]
    dtype = x_prompt.dtype
    meta = jnp.broadcast_to(meta_tokens.astype(dtype)[None], (bp, N_META, D_MODEL))
    xp = jnp.concatenate([meta, x_prompt], axis=1)
    xs = x_sample

    def sample_attend(ck, cv):
        return lambda q, k, v, bias: sb_sample(q, k, v, bias, ck, cv, page_table)

    outs_p = []
    outs_s = []
    for l in range(DEPTH):
        p = {
            'norm1': norm1[l], 'w_in': w_in[l], 'ssd_conv_w': ssd_conv_w[l], 'ssd_conv_b': ssd_conv_b[l],
            'ssd_dt_bias': ssd_dt_bias[l], 'ssd_a_log': ssd_a_log[l], 'ssd_d': ssd_d[l], 'ssd_norm': ssd_norm[l],
            'q_norm': q_norm[l], 'k_norm': k_norm[l], 'sb_bias': sb_bias[l], 'sb_out_norm': sb_out_norm[l],
            'rg_conv_w': rg_conv_w[l], 'rg_conv_b': rg_conv_b[l], 'rg_wa': rg_wa[l], 'rg_ba': rg_ba[l],
            'rg_wx': rg_wx[l], 'rg_bx': rg_bx[l], 'rg_lambda': rg_lambda[l], 'rg_out_norm': rg_out_norm[l],
            'w_out': w_out[l], 'norm2': norm2[l], 'w_up': w_up[l], 'ffn_conv_w': ffn_conv_w[l],
            'ffn_conv_b': ffn_conv_b[l], 'w_down': w_down[l],
        }
        xp, st_p = layer(
            xp, p,
            jnp.zeros((bp, SSD_CONV - 1, SSD_XBC), dtype),
            jnp.zeros((bp, SSD_HEADS, HEAD_DIM, SSD_STATE), jnp.float32),
            jnp.zeros((bp, RG_CONV - 1, RG_WIDTH), dtype),
            jnp.zeros((bp, RG_WIDTH), jnp.float32),
            jnp.zeros((bp, FFN_CONV - 1, D_FF), dtype),
            ssd_prompt_scan, sb_prompt)
        xs, st_s = layer(
            xs, p, state_ssm_conv[l], state_ssm[l], state_rg_conv[l], state_rg[l], state_ffn_conv[l],
            ssd_sample_scan, sample_attend(cache_k[l], cache_v[l]))
        outs_p.append(st_p)
        outs_s.append(st_s)

    def stk(outs, i):
        return jnp.stack([o[i] for o in outs], axis=0)

    y_prompt = xp[:, N_META:]
    y_sample = xs
    return (y_prompt, y_sample,
            stk(outs_p, 0), stk(outs_p, 1), stk(outs_p, 2), stk(outs_p, 3), stk(outs_p, 4), stk(outs_p, 5), stk(outs_p, 6),
            stk(outs_s, 0), stk(outs_s, 1), stk(outs_s, 2), stk(outs_s, 3), stk(outs_s, 4), stk(outs_s, 5), stk(outs_s, 6))
```

```python
import functools

import jax
import jax.numpy as jnp
import numpy as np
from jax import lax
from jax.experimental import pallas as pl
from jax.experimental.pallas import tpu as pltpu

F32 = jnp.float32
BF16 = jnp.bfloat16

D_MODEL = 1024
N_META = 16
HEAD_DIM = 64
SSD_WIDTH = 384
SSD_HEADS = 6
SSD_STATE = 128
SSD_XBC = 896
SB_WIDTH = 384
SB_HEADS = 6
RG_WIDTH = 256
RG_C = 8.0
D_FF = 2816
EPS = 1e-6

LANES = 128
SUBLANES = 8
PAIR = 2 * HEAD_DIM
DT_COLS = LANES
IN_COLS_PAD = 2 * SSD_WIDTH + 512 + 3 * SB_WIDTH + 2 * RG_WIDTH + DT_COLS
NEG_BIG = -1e30

ROW_TILE = 256
CHUNK = 128
ATT_TILE = 256
PAGE = 128
PAGES_PER_STEP = 8
VMEM_LIMIT = 56 * 1024 * 1024

NT_DIMS = (((1,), (1,)), ((), ()))


def _dot(a, b):
    return jnp.dot(a, b, preferred_element_type=F32)


def _dot_nt(a, b):
    return lax.dot_general(a, b, NT_DIMS, preferred_element_type=F32)


def _split_bf16(x, n):
    parts = []
    r = x
    for _ in range(n):
        p = r.astype(BF16)
        parts.append(p)
        r = r - p.astype(F32)
    return parts


def _dot_split_lhs(x, m, n):
    out = None
    for p in _split_bf16(x, n):
        t = _dot(p, m)
        out = t if out is None else out + t
    return out


def _dot_split_rhs(m, x, n):
    out = None
    for p in _split_bf16(x, n):
        t = _dot(m, p)
        out = t if out is None else out + t
    return out


def _rms(x, g):
    return x * lax.rsqrt(jnp.mean(x * x, axis=-1, keepdims=True) + EPS) * g


def _softplus(x):
    return jnp.maximum(x, 0.0) + jnp.log(1.0 + jnp.exp(-jnp.abs(x)))


def _silu(x):
    return x * jax.nn.sigmoid(x)


def _gelu_tanh(x):
    return 0.5 * x * (1.0 + jnp.tanh(0.7978845608028654 * (x + 0.044715 * (x * x * x))))


def _iota(shape, dim):
    return lax.broadcasted_iota(jnp.int32, shape, dim)


def _inproj_kernel(x_ref, g_ref, w_ref, bd_ref, qn_ref, kn_ref,
                   z_ref, xbc_ref, dt_ref, qb_ref, kb_ref, vb_ref, k_ref, v_ref, xr_ref, gr_ref,
                   *, tm, p0):
    x = x_ref[0]
    h = _rms(x, g_ref[...])
    if p0 > 0:
        rows = pl.program_id(1) * tm + _iota((tm, 1), 0)
        h = jnp.where(rows >= p0, h, 0.0)
    proj = _dot(h.astype(BF16), w_ref[...])
    z_ref[0] = proj[:, 0:384]
    xbc_ref[0] = proj[:, 384:1280]
    q = proj[:, 1280:1664]
    k = proj[:, 1664:2048]
    v = proj[:, 2048:2432]
    xr_ref[0] = proj[:, 2432:2688]
    gr_ref[0] = proj[:, 2688:2944]
    dt_ref[0] = proj[:, 2944:3072]
    bd = bd_ref[...]
    q_ms = _dot_split_lhs(q * q, bd, 2) * (1.0 / HEAD_DIM)
    k_ms = _dot_split_lhs(k * k, bd, 2) * (1.0 / HEAD_DIM)
    qn = q * lax.rsqrt(q_ms + EPS) * qn_ref[...]
    kn = k * lax.rsqrt(k_ms + EPS) * kn_ref[...]
    qb_ref[0] = (qn * (HEAD_DIM ** -0.5)).astype(BF16)
    kb_ref[0] = kn.astype(BF16)
    vb_ref[0] = v.astype(BF16)
    k_ref[0] = kn
    v_ref[0] = v


def _inproj(x, g, w, bd, qn, kn, *, tm, p0):
    nb, rows, _ = x.shape
    grid = (nb, rows // tm)

    def tile(width):
        return pl.BlockSpec((1, tm, width), lambda b, i: (b, i, 0))

    def const(shape):
        return pl.BlockSpec(shape, lambda b, i: (0,) * len(shape), pipeline_mode=pl.Buffered(1))

    widths = (384, 896, DT_COLS, 384, 384, 384, 384, 384, 256, 256)
    dtypes = (F32, F32, F32, BF16, BF16, BF16, F32, F32, F32, F32)
    return pl.pallas_call(
        functools.partial(_inproj_kernel, tm=tm, p0=p0),
        grid=grid,
        in_specs=[tile(D_MODEL), const((1, D_MODEL)), const((D_MODEL, IN_COLS_PAD)),
                  const((384, 384)), const((1, 384)), const((1, 384))],
        out_specs=[tile(wd) for wd in widths],
        out_shape=[jax.ShapeDtypeStruct((nb, rows, wd), dt) for wd, dt in zip(widths, dtypes)],
        compiler_params=pltpu.CompilerParams(
            dimension_semantics=("parallel", "parallel"), vmem_limit_bytes=VMEM_LIMIT),
        name="inproj",
    )(x, g, w, bd, qn, kn)


def _ssd_kernel(xbc_ref, z_ref, dt_ref, cw_ref, cb_ref, dtb_ref, alog_ref, dvec_ref, nw_ref,
                y_ref, st_ref, tail_ref, xbuf, hst, *, c, p0):
    ci = pl.program_id(1)

    @pl.when(ci == 0)
    def _():
        xbuf[0:SUBLANES, :] = jnp.zeros((SUBLANES, SSD_XBC), F32)
        hst[...] = jnp.zeros_like(hst)

    xbuf[SUBLANES:SUBLANES + c, :] = xbc_ref[0]
    conv = cb_ref[...]
    for j in range(4):
        conv = conv + xbuf[5 + j:5 + j + c, :] * cw_ref[j:j + 1, :]
    last = xbuf[c:c + SUBLANES, :]
    tail_ref[0] = last
    xbuf[0:SUBLANES, :] = last
    act = _silu(conv)
    xs = act[:, 0:384]
    bm = act[:, 384:640]
    cm = act[:, 640:896]

    rows = ci * c + _iota((c, 1), 0)
    dt = _softplus(dt_ref[0] + dtb_ref[...])
    dt = jnp.where(rows >= p0, dt, 0.0)
    a = dt * (-jnp.exp(alog_ref[...]))
    r_i = _iota((c, c), 0)
    c_i = _iota((c, c), 1)
    tri = r_i >= c_i
    cum = _dot_split_rhs(tri.astype(BF16), a, 3)
    cum_t = cum.T
    cum_last = cum[c - 1:c, :]
    e_cum = jnp.exp(cum)
    dec = jnp.exp(cum_last - cum) * dt
    e_last = jnp.exp(cum_last)

    lane = _iota((1, LANES), 1)
    lo_half = lane < HEAD_DIM
    rowp = _iota((LANES, 1), 0)
    top_half = rowp < HEAD_DIM
    cbs = [_dot_nt(cm[:, g * LANES:(g + 1) * LANES], bm[:, g * LANES:(g + 1) * LANES]) for g in range(2)]

    def col(m, h):
        return m[:, h:h + 1]

    ys = []
    for pr in range(3):
        ha, hb = 2 * pr, 2 * pr + 1
        ga, gb = ha // 3, hb // 3
        xp = xs[:, pr * LANES:(pr + 1) * LANES]
        xd = xp * jnp.where(lo_half, col(dt, ha), col(dt, hb))
        hpair = hst[pr * LANES:(pr + 1) * LANES, :]
        y_intra = []
        for h, g in ((ha, ga), (hb, gb)):
            seg = col(cum, h) - cum_t[h:h + 1, :]
            lm = jnp.exp(jnp.where(tri, seg, -jnp.inf))
            y_intra.append(_dot(cbs[g] * lm, xd))
        yi_a = _dot_nt(cm[:, ga * LANES:(ga + 1) * LANES], hpair)
        yi_b = yi_a if gb == ga else _dot_nt(cm[:, gb * LANES:(gb + 1) * LANES], hpair)
        y = jnp.where(lo_half, y_intra[0] + yi_a * col(e_cum, ha), y_intra[1] + yi_b * col(e_cum, hb))
        ys.append(y + xp * dvec_ref[:, pr * LANES:(pr + 1) * LANES])
        xe_t = (xp * jnp.where(lo_half, col(dec, ha), col(dec, hb))).T
        hn_a = _dot(xe_t, bm[:, ga * LANES:(ga + 1) * LANES])
        hn_b = hn_a if gb == ga else _dot(xe_t, bm[:, gb * LANES:(gb + 1) * LANES])
        keep = jnp.where(top_half, col(e_last, ha), col(e_last, hb))
        hst[pr * LANES:(pr + 1) * LANES, :] = hpair * keep + jnp.where(top_half, hn_a, hn_b)

    y = jnp.concatenate(ys, axis=-1) * _silu(z_ref[0])
    y_ref[0] = _rms(y, nw_ref[...]).astype(BF16)

    @pl.when(ci == pl.num_programs(1) - 1)
    def _():
        st_ref[0] = hst[...]


def _ssd(xbc, z, dt, cw, cb, dtb, alog, dvec, nw, *, c, p0):
    nb, rows, _ = xbc.shape

    def tile(width):
        return pl.BlockSpec((1, c, width), lambda b, i: (b, i, 0))

    def const(shape):
        return pl.BlockSpec(shape, lambda b, i: (0,) * len(shape))

    return pl.pallas_call(
        functools.partial(_ssd_kernel, c=c, p0=p0),
        grid=(nb, rows // c),
        in_specs=[tile(SSD_XBC), tile(SSD_WIDTH), tile(DT_COLS), const((4, SSD_XBC)), const((1, SSD_XBC)),
                  const((1, DT_COLS)), const((1, DT_COLS)), const((1, SSD_WIDTH)), const((1, SSD_WIDTH))],
        out_specs=[tile(SSD_WIDTH),
                   pl.BlockSpec((1, SSD_WIDTH, SSD_STATE), lambda b, i: (b, 0, 0)),
                   pl.BlockSpec((1, SUBLANES, SSD_XBC), lambda b, i: (b, 0, 0))],
        out_shape=[jax.ShapeDtypeStruct((nb, rows, SSD_WIDTH), BF16),
                   jax.ShapeDtypeStruct((nb, SSD_WIDTH, SSD_STATE), F32),
                   jax.ShapeDtypeStruct((nb, SUBLANES, SSD_XBC), F32)],
        scratch_shapes=[pltpu.VMEM((SUBLANES + c, SSD_XBC), F32), pltpu.VMEM((SSD_WIDTH, SSD_STATE), F32)],
        compiler_params=pltpu.CompilerParams(
            dimension_semantics=("arbitrary", "arbitrary"), vmem_limit_bytes=VMEM_LIMIT),
        name="ssd_scan",
    )(xbc, z, dt, cw, cb, dtb, alog, dvec, nw)


def _rg_gates(xc, wax_ref, bax_ref, lam_ref):
    ri = jax.nn.sigmoid(_dot(xc, wax_ref[...]) + bax_ref[...])
    r = ri[:, 0:RG_WIDTH]
    i = ri[:, RG_WIDTH:2 * RG_WIDTH]
    log_a = RG_C * r * (-_softplus(-lam_ref[...]))
    a = jnp.exp(log_a)
    u = jnp.sqrt(1.0 - jnp.exp(2.0 * log_a)) * (i * xc)
    return a, u


def _rg_kernel(xr_ref, gr_ref, cw_ref, cb_ref, wax_ref, bax_ref, lam_ref, nw_ref,
               y_ref, h_ref, tail_ref, xbuf, hprev, *, c, p0):
    ci = pl.program_id(1)

    @pl.when(ci == 0)
    def _():
        xbuf[0:SUBLANES, :] = jnp.zeros((SUBLANES, RG_WIDTH), F32)
        hprev[...] = jnp.zeros_like(hprev)

    xbuf[SUBLANES:SUBLANES + c, :] = xr_ref[0]
    xc = cb_ref[...]
    for j in range(4):
        xc = xc + xbuf[5 + j:5 + j + c, :] * cw_ref[j:j + 1, :]
    last = xbuf[c:c + SUBLANES, :]
    tail_ref[0] = last
    xbuf[0:SUBLANES, :] = last

    a, u = _rg_gates(xc, wax_ref, bax_ref, lam_ref)
    rowi = _iota((c, 1), 0)
    valid = (ci * c + rowi) >= p0
    a = jnp.where(valid, a, 1.0)
    u = jnp.where(valid, u, 0.0)
    d = 1
    while d < c:
        a_sh = jnp.where(rowi >= d, pltpu.roll(a, d, 0), 1.0)
        u_sh = jnp.where(rowi >= d, pltpu.roll(u, d, 0), 0.0)
        u = a * u_sh + u
        a = a * a_sh
        d *= 2
    h = a * hprev[...] + u
    h_last = h[c - 1:c, :]
    hprev[...] = h_last
    h_ref[0] = h_last
    y = h * _gelu_tanh(gr_ref[0])
    y_ref[0] = _rms(y, nw_ref[...]).astype(BF16)


def _rg(xr, gr, cw, cb, wax, bax, lam, nw, *, c, p0):
    nb, rows, _ = xr.shape

    def tile():
        return pl.BlockSpec((1, c, RG_WIDTH), lambda b, i: (b, i, 0))

    def const(shape):
        return pl.BlockSpec(shape, lambda b, i: (0,) * len(shape))

    return pl.pallas_call(
        functools.partial(_rg_kernel, c=c, p0=p0),
        grid=(nb, rows // c),
        in_specs=[tile(), tile(), const((4, RG_WIDTH)), const((1, RG_WIDTH)), const((RG_WIDTH, 2 * RG_WIDTH)),
                  const((1, 2 * RG_WIDTH)), const((1, RG_WIDTH)), const((1, RG_WIDTH))],
        out_specs=[tile(),
                   pl.BlockSpec((1, 1, RG_WIDTH), lambda b, i: (b, 0, 0)),
                   pl.BlockSpec((1, SUBLANES, RG_WIDTH), lambda b, i: (b, 0, 0))],
        out_shape=[jax.ShapeDtypeStruct((nb, rows, RG_WIDTH), BF16),
                   jax.ShapeDtypeStruct((nb, 1, RG_WIDTH), F32),
                   jax.ShapeDtypeStruct((nb, SUBLANES, RG_WIDTH), F32)],
        scratch_shapes=[pltpu.VMEM((SUBLANES + c, RG_WIDTH), F32), pltpu.VMEM((1, RG_WIDTH), F32)],
        compiler_params=pltpu.CompilerParams(dimension_semantics=("arbitrary", "arbitrary")),
        name="rglru_scan",
    )(xr, gr, cw, cb, wax, bax, lam, nw)


def _sb_tile(z, incl_mat, carry):
    e = jnp.exp(-jnp.abs(z))
    l = jnp.minimum(-z, 0.0) - jnp.log(1.0 + e)
    incl = _dot_split_lhs(l, incl_mat, 2)
    w = jnp.exp(z + incl + carry)
    return w, carry + incl[:, 0:1]


def _attn_kernel(q_ref, k_ref, v_ref, kb_ref, o_ref, *, t):
    qi = pl.program_id(2)
    q = q_ref[0]
    lane = _iota((1, PAIR), 1)
    lo_half = lane < HEAD_DIM
    zero = jnp.zeros((), BF16)
    qs = (jnp.where(lo_half, q, zero), jnp.where(lo_half, zero, q))
    r_i = _iota((t, t), 0)
    c_i = _iota((t, t), 1)
    incl_mat = (r_i >= c_i).astype(BF16)
    causal = jnp.where(c_i < r_i, 0.0, NEG_BIG)

    def tile(j, carry, diag):
        ca, cb, acc = carry
        start = pl.multiple_of(j * t, t)
        ks = k_ref[0, pl.ds(start, t), :]
        vs = v_ref[0, pl.ds(start, t), :]
        vh = (jnp.where(lo_half, vs, zero), jnp.where(lo_half, zero, vs))
        kbias = kb_ref[0, j]
        cs = []
        for hh, c in enumerate((ca, cb)):
            z = _dot_nt(qs[hh], ks) + kbias[hh:hh + 1, :]
            if diag:
                z = z + causal
            w, c_new = _sb_tile(z, incl_mat, c)
            acc = acc + _dot(w.astype(BF16), vh[hh])
            cs.append(c_new)
        return cs[0], cs[1], acc

    carry = (jnp.zeros((t, 1), F32), jnp.zeros((t, 1), F32), jnp.zeros((t, PAIR), F32))
    carry = tile(qi, carry, True)
    carry = lax.fori_loop(0, qi, lambda jj, cr: tile(qi - 1 - jj, cr, False), carry)
    o_ref[0] = carry[2]


def _attn(qb, kb, vb, kbias, *, t):
    nb, rows, _ = qb.shape
    nk = rows // t
    return pl.pallas_call(
        functools.partial(_attn_kernel, t=t),
        grid=(nb, SB_WIDTH // PAIR, nk),
        in_specs=[pl.BlockSpec((1, t, PAIR), lambda b, p, i: (b, i, p)),
                  pl.BlockSpec((1, rows, PAIR), lambda b, p, i: (b, 0, p)),
                  pl.BlockSpec((1, rows, PAIR), lambda b, p, i: (b, 0, p)),
                  pl.BlockSpec((1, nk, SUBLANES, t), lambda b, p, i: (p, 0, 0, 0))],
        out_specs=pl.BlockSpec((1, t, PAIR), lambda b, p, i: (b, i, p)),
        out_shape=jax.ShapeDtypeStruct((nb, rows, SB_WIDTH), F32),
        compiler_params=pltpu.CompilerParams(
            dimension_semantics=("parallel", "parallel", "parallel"), vmem_limit_bytes=VMEM_LIMIT),
        name="sb_attention",
    )(qb, kb, vb, kbias)


def _ffn_kernel(x_ref, yssd_ref, ysb_ref, yrg_ref, sbn_ref, wo_ref, n2_ref, wup_ref, cw_ref, cb_ref, wd_ref,
                out_ref, tail_ref, gbuf, *, tm, p0):
    i = pl.program_id(1)

    @pl.when(i == 0)
    def _():
        gbuf[0:SUBLANES, :] = jnp.zeros((SUBLANES, D_FF), F32)

    ysb = _rms(ysb_ref[0], sbn_ref[...]).astype(BF16)
    mix = jnp.concatenate([yssd_ref[0], ysb, yrg_ref[0]], axis=-1)
    xm = x_ref[0] + _dot(mix, wo_ref[...])
    h2 = _rms(xm, n2_ref[...])
    if p0 > 0:
        rows = i * tm + _iota((tm, 1), 0)
        h2 = jnp.where(rows >= p0, h2, 0.0)
    gu = _dot(h2.astype(BF16), wup_ref[...])
    gbuf[SUBLANES:SUBLANES + tm, :] = gu[:, 0:D_FF]
    conv = cb_ref[...]
    for j in range(3):
        conv = conv + gbuf[6 + j:6 + j + tm, :] * cw_ref[j:j + 1, :]
    last = gbuf[tm:tm + SUBLANES, :]
    tail_ref[0] = last
    gbuf[0:SUBLANES, :] = last
    act = (_silu(conv) * gu[:, D_FF:2 * D_FF]).astype(BF16)
    out_ref[0] = xm + _dot(act, wd_ref[...])


def _ffn(x, yssd, ysb, yrg, sbn, wo, n2, wup, cw, cb, wd, *, tm, p0):
    nb, rows, _ = x.shape

    def tile(width):
        return pl.BlockSpec((1, tm, width), lambda b, i: (b, i, 0))

    def const(shape):
        return pl.BlockSpec(shape, lambda b, i: (0,) * len(shape), pipeline_mode=pl.Buffered(1))

    return pl.pallas_call(
        functools.partial(_ffn_kernel, tm=tm, p0=p0),
        grid=(nb, rows // tm),
        in_specs=[tile(D_MODEL), tile(SSD_WIDTH), tile(SB_WIDTH), tile(RG_WIDTH), const((1, SB_WIDTH)),
                  const((D_MODEL, D_MODEL)), const((1, D_MODEL)), const((D_MODEL, 2 * D_FF)),
                  const((3, D_FF)), const((1, D_FF)), const((D_FF, D_MODEL))],
        out_specs=[tile(D_MODEL), pl.BlockSpec((1, SUBLANES, D_FF), lambda b, i: (b, 0, 0))],
        out_shape=[jax.ShapeDtypeStruct((nb, rows, D_MODEL), F32),
                   jax.ShapeDtypeStruct((nb, SUBLANES, D_FF), F32)],
        scratch_shapes=[pltpu.VMEM((SUBLANES + tm, D_FF), F32)],
        compiler_params=pltpu.CompilerParams(
            dimension_semantics=("arbitrary", "arbitrary"), vmem_limit_bytes=VMEM_LIMIT),
        name="outproj_ffn",
    )(x, yssd, ysb, yrg, sbn, wo, n2, wup, cw, cb, wd)


def _sample_mix_kernel(z_ref, xbc_ref, dt_ref, sbuf_ref, st_ref, xr_ref, gr_ref, rbuf_ref, rh_ref,
                       scw_ref, scb_ref, dtb_ref, alog_ref, dvec_ref, snw_ref,
                       rcw_ref, rcb_ref, wax_ref, bax_ref, lam_ref, rnw_ref,
                       yssd_ref, st_out_ref, sbuf_out_ref, yrg_ref, rh_out_ref, rbuf_out_ref,
                       yscr, *, nseq):
    xbc = xbc_ref[...]
    conv = scb_ref[...] + xbc * scw_ref[3:4, :]
    for j in range(3):
        conv = conv + sbuf_ref[j] * scw_ref[j:j + 1, :]
    sbuf_out_ref[0] = sbuf_ref[1]
    sbuf_out_ref[1] = sbuf_ref[2]
    sbuf_out_ref[2] = xbc
    act = _silu(conv)
    xs = act[:, 0:384]
    bm = act[:, 384:640]
    cm = act[:, 640:896]
    dt = _softplus(dt_ref[...] + dtb_ref[...])
    da = jnp.exp(dt * (-jnp.exp(alog_ref[...])))

    lane = _iota((1, LANES), 1)
    lo_half = lane < HEAD_DIM
    rowp = _iota((LANES, 1), 0)
    top_half = rowp < HEAD_DIM
    pad_rows = LANES - nseq
    yscr[...] = jnp.zeros_like(yscr)
    for pr in range(3):
        ha, hb = 2 * pr, 2 * pr + 1
        ga, gb = ha // 3, hb // 3
        xp = xs[:, pr * LANES:(pr + 1) * LANES]
        xd = xp * jnp.where(lo_half, dt[:, ha:ha + 1], dt[:, hb:hb + 1])
        xd_t = jnp.concatenate([xd, jnp.zeros((pad_rows, LANES), F32)], axis=0).T
        for b in range(nseq):
            hpair = st_ref[b, pr * LANES:(pr + 1) * LANES, :]
            keep = jnp.where(top_half, da[b:b + 1, ha:ha + 1], da[b:b + 1, hb:hb + 1])
            brow = jnp.where(top_half, bm[b:b + 1, ga * LANES:(ga + 1) * LANES],
                             bm[b:b + 1, gb * LANES:(gb + 1) * LANES])
            hnew = hpair * keep + xd_t[:, b:b + 1] * brow
            st_out_ref[b, pr * LANES:(pr + 1) * LANES, :] = hnew
            crow = jnp.where(top_half, cm[b:b + 1, ga * LANES:(ga + 1) * LANES],
                             cm[b:b + 1, gb * LANES:(gb + 1) * LANES])
            ycol = jnp.sum(hnew * crow, axis=1, keepdims=True)
            yscr[pr, :, b:b + 1] = ycol
    ys = []
    for pr in range(3):
        ys.append(yscr[pr].T[0:nseq, :])
    y = jnp.concatenate(ys, axis=-1) + xs * dvec_ref[...]
    y = y * _silu(z_ref[...])
    yssd_ref[...] = _rms(y, snw_ref[...]).astype(BF16)

    xr = xr_ref[...]
    xc = rcb_ref[...] + xr * rcw_ref[3:4, :]
    for j in range(3):
        xc = xc + rbuf_ref[j] * rcw_ref[j:j + 1, :]
    rbuf_out_ref[0] = rbuf_ref[1]
    rbuf_out_ref[1] = rbuf_ref[2]
    rbuf_out_ref[2] = xr
    a, u = _rg_gates(xc, wax_ref, bax_ref, lam_ref)
    h = a * rh_ref[...] + u
    rh_out_ref[...] = h
    yrg_ref[...] = _rms(h * _gelu_tanh(gr_ref[...]), rnw_ref[...]).astype(BF16)


def _sample_mix(z, xbc, dt, sbuf, st, xr, gr, rbuf, rh, scw, scb, dtb, alog, dvec, snw,
                rcw, rcb, wax, bax, lam, rnw):
    nseq = z.shape[0]
    return pl.pallas_call(
        functools.partial(_sample_mix_kernel, nseq=nseq),
        out_shape=[jax.ShapeDtypeStruct((nseq, SSD_WIDTH), BF16),
                   jax.ShapeDtypeStruct((nseq, SSD_WIDTH, SSD_STATE), F32),
                   jax.ShapeDtypeStruct((3, nseq, SSD_XBC), F32),
                   jax.ShapeDtypeStruct((nseq, RG_WIDTH), BF16),
                   jax.ShapeDtypeStruct((nseq, RG_WIDTH), F32),
                   jax.ShapeDtypeStruct((3, nseq, RG_WIDTH), F32)],
        scratch_shapes=[pltpu.VMEM((3, LANES, LANES), F32)],
        compiler_params=pltpu.CompilerParams(vmem_limit_bytes=VMEM_LIMIT),
        name="sample_mixers",
    )(z, xbc, dt, sbuf, st, xr, gr, rbuf, rh, scw, scb, dtb, alog, dvec, snw, rcw, rcb, wax, bax, lam, rnw)


def _sample_attn_kernel(pt_ref, q_ref, bias_ref, *refs, npg):
    del pt_ref
    k_refs = refs[0:npg]
    v_refs = refs[npg:2 * npg]
    o_ref = refs[2 * npg]
    carry_ref, acc_ref = refs[2 * npg + 1:]
    g = pl.program_id(1)

    @pl.when(g == 0)
    def _():
        carry_ref[...] = jnp.zeros_like(carry_ref)
        acc_ref[...] = jnp.zeros_like(acc_ref)

    hrow = _iota((SUBLANES, SB_WIDTH), 0)
    hlane = _iota((SUBLANES, SB_WIDTH), 1) // HEAD_DIM
    own = hrow == hlane
    qbd = jnp.where(own, jnp.broadcast_to(q_ref[0], (SUBLANES, SB_WIDTH)), 0.0)
    r_i = _iota((PAGE, PAGE), 0)
    c_i = _iota((PAGE, PAGE), 1)
    incl_mat = (r_i >= c_i).astype(BF16)
    carry = carry_ref[...]
    acc = acc_ref[...]
    for r in range(npg):
        z = _dot_nt(qbd, k_refs[r][0]) + bias_ref[...]
        w, carry = _sb_tile(z, incl_mat, carry)
        acc = acc + _dot(w, v_refs[r][0])
    carry_ref[...] = carry
    acc_ref[...] = acc

    @pl.when(g == pl.num_programs(1) - 1)
    def _():
        o_ref[0] = jnp.sum(jnp.where(own, acc, 0.0), axis=0, keepdims=True)


def _sample_attn(pt_flat, q3, bias8, ck, cv, *, nseq, n_pages, npg):
    steps = n_pages // npg

    def page_spec(r):
        def imap(b, g, pt):
            return (pt[b * n_pages + (n_pages - 1 - (g * npg + r))], 0, 0)
        return pl.BlockSpec((1, PAGE, SB_WIDTH), imap)

    grid_spec = pltpu.PrefetchScalarGridSpec(
        num_scalar_prefetch=1,
        grid=(nseq, steps),
        in_specs=[pl.BlockSpec((1, 1, SB_WIDTH), lambda b, g, pt: (b, 0, 0)),
                  pl.BlockSpec((SUBLANES, PAGE), lambda b, g, pt: (0, 0))]
                 + [page_spec(r) for r in range(npg)] + [page_spec(r) for r in range(npg)],
        out_specs=pl.BlockSpec((1, 1, SB_WIDTH), lambda b, g, pt: (b, 0, 0)),
        scratch_shapes=[pltpu.VMEM((SUBLANES, 1), F32), pltpu.VMEM((SUBLANES, SB_WIDTH), F32)],
    )
    return pl.pallas_call(
        functools.partial(_sample_attn_kernel, npg=npg),
        grid_spec=grid_spec,
        out_shape=jax.ShapeDtypeStruct((nseq, 1, SB_WIDTH), F32),
        compiler_params=pltpu.CompilerParams(dimension_semantics=("arbitrary", "arbitrary")),
        name="sample_attention",
    )(pt_flat, q3, bias8, *([ck] * npg), *([cv] * npg))


def _sample_ffn_kernel(x_ref, yssd_ref, ysb_ref, yrg_ref, fbuf_ref, sbn_ref, wo_ref, n2_ref, wup_ref,
                       cw_ref, cb_ref, wd_ref, out_ref, fbuf_out_ref):
    ysb = _rms(ysb_ref[...], sbn_ref[...]).astype(BF16)
    mix = jnp.concatenate([yssd_ref[...], ysb, yrg_ref[...]], axis=-1)
    xm = x_ref[...] + _dot(mix, wo_ref[...])
    h2 = _rms(xm, n2_ref[...])
    gu = _dot(h2.astype(BF16), wup_ref[...])
    g = gu[:, 0:D_FF]
    conv = cb_ref[...] + fbuf_ref[0] * cw_ref[0:1, :] + fbuf_ref[1] * cw_ref[1:2, :] + g * cw_ref[2:3, :]
    fbuf_out_ref[0] = fbuf_ref[1]
    fbuf_out_ref[1] = g
    act = (_silu(conv) * gu[:, D_FF:2 * D_FF]).astype(BF16)
    out_ref[...] = xm + _dot(act, wd_ref[...])


def _sample_ffn(x, yssd, ysb, yrg, fbuf, sbn, wo, n2, wup, cw, cb, wd):
    nseq = x.shape[0]
    return pl.pallas_call(
        _sample_ffn_kernel,
        out_shape=[jax.ShapeDtypeStruct((nseq, D_MODEL), F32),
                   jax.ShapeDtypeStruct((2, nseq, D_FF), F32)],
        compiler_params=pltpu.CompilerParams(vmem_limit_bytes=VMEM_LIMIT),
        name="sample_outproj_ffn",
    )(x, yssd, ysb, yrg, fbuf, sbn, wo, n2, wup, cw, cb, wd)


def _pad_cols(a, width):
    return jnp.pad(a, ((0, 0), (0, width - a.shape[-1])))


def _block_diag(w):
    nblk, bi, bj = w.shape
    out = jnp.zeros((nblk * bi, nblk * bj), w.dtype)
    for h in range(nblk):
        out = out.at[h * bi:(h + 1) * bi, h * bj:(h + 1) * bj].set(w[h])
    return out


def kernel(x_prompt, x_sample, cache_k, cache_v, page_table, state_ssm, state_ssm_conv, state_rg, state_rg_conv, state_ffn_conv, meta_tokens, norm1, w_in, ssd_conv_w, ssd_conv_b, ssd_dt_bias, ssd_a_log, ssd_d, ssd_norm, q_norm, k_norm, sb_bias, sb_out_norm, rg_conv_w, rg_conv_b, rg_wa, rg_ba, rg_wx, rg_bx, rg_lambda, rg_out_norm, w_out, norm2, w_up, ffn_conv_w, ffn_conv_b, w_down):
    depth = w_in.shape[0]
    nb, seq, _ = x_prompt.shape
    nseq = x_sample.shape[0]
    n_pool = cache_k.shape[1]
    n_pages = page_table.shape[1]
    t_real = N_META + seq
    t_pad = -(-t_real // ATT_TILE) * ATT_TILE
    p0 = t_pad - t_real
    nk = t_pad // ATT_TILE

    meta = jnp.broadcast_to(meta_tokens[None], (nb, N_META, D_MODEL))
    xp = jnp.concatenate([jnp.zeros((nb, p0, D_MODEL), F32), meta, x_prompt], axis=1)
    xs = x_sample.reshape(1, nseq, D_MODEL)

    head_of = np.arange(SB_WIDTH) // HEAD_DIM
    bd = jnp.asarray(head_of[:, None] == head_of[None, :], BF16)
    key_valid = jnp.arange(t_pad) >= p0
    ck = cache_k.reshape(depth * n_pool, PAGE, SB_WIDTH)
    cv = cache_v.reshape(depth * n_pool, PAGE, SB_WIDTH)

    outs_p, outs_s = [], []
    for l in range(depth):
        wz, wxbc, wdt, wq, wk, wv, wxr, wgr = jnp.split(
            w_in[l], np.cumsum([384, 896, 6, 384, 384, 384, 256])[:].tolist(), axis=1)
        w1 = jnp.concatenate([wz, wxbc, wq, wk, wv, wxr, wgr, _pad_cols(wdt, DT_COLS)], axis=1).astype(BF16)
        g1 = norm1[l][None]
        qn = jnp.tile(q_norm[l], SB_HEADS)[None]
        kn = jnp.tile(k_norm[l], SB_HEADS)[None]
        dtb = _pad_cols(ssd_dt_bias[l][None], DT_COLS)
        alog = _pad_cols(ssd_a_log[l][None], DT_COLS)
        dvec = jnp.repeat(ssd_d[l], HEAD_DIM)[None]
        snw = ssd_norm[l][None]
        scw, scb = ssd_conv_w[l], ssd_conv_b[l][None]
        rcw, rcb = rg_conv_w[l], rg_conv_b[l][None]
        wax = jnp.concatenate([_block_diag(rg_wa[l]), _block_diag(rg_wx[l])], axis=1)
        bax = jnp.concatenate([rg_ba[l].reshape(1, RG_WIDTH), rg_bx[l].reshape(1, RG_WIDTH)], axis=1)
        lam = rg_lambda[l][None]
        rnw = rg_out_norm[l][None]
        sbn = sb_out_norm[l][None]
        wo = w_out[l].astype(BF16)
        n2 = norm2[l][None]
        wup = w_up[l].astype(BF16)
        fcw, fcb = ffn_conv_w[l], ffn_conv_b[l][None]
        wd = w_down[l].astype(BF16)

        z, xbc, dt, qb, kb, vb, kf, vf, xr, gr = _inproj(xp, g1, w1, bd, qn, kn, tm=ROW_TILE, p0=p0)
        yssd, ssm_p, sconv_p = _ssd(xbc, z, dt, scw, scb, dtb, alog, dvec, snw, c=CHUNK, p0=p0)
        yrg, rgh_p, rconv_p = _rg(xr, gr, rcw, rcb, wax, bax, lam, rnw, c=CHUNK, p0=p0)
        kbias = jnp.where(key_valid[None, :], sb_bias[l][:, None], NEG_BIG)
        kbias = kbias.reshape(3, 2, nk, ATT_TILE).transpose(0, 2, 1, 3)
        kbias = jnp.pad(kbias, ((0, 0), (0, 0), (0, SUBLANES - 2), (0, 0)))
        ysb = _attn(qb, kb, vb, kbias, t=ATT_TILE)
        xp, fconv_p = _ffn(xp, yssd, ysb, yrg, sbn, wo, n2, wup, fcw, fcb, wd, tm=ROW_TILE, p0=p0)
        outs_p.append((kf[:, p0:].reshape(nb, t_real, SB_HEADS, HEAD_DIM),
                       vf[:, p0:].reshape(nb, t_real, SB_HEADS, HEAD_DIM),
                       ssm_p.reshape(nb, SSD_HEADS, HEAD_DIM, SSD_STATE),
                       sconv_p[:, SUBLANES - 3:],
                       rgh_p.reshape(nb, RG_WIDTH),
                       rconv_p[:, SUBLANES - 3:],
                       fconv_p[:, SUBLANES - 2:]))

        z, xbc, dt, qb, kb, vb, kf, vf, xr, gr = _inproj(xs, g1, w1, bd, qn, kn, tm=nseq, p0=0)
        yssd, ssm_s, sconv_s, yrg, rgh_s, rconv_s = _sample_mix(
            z[0], xbc[0], dt[0], state_ssm_conv[l].transpose(1, 0, 2),
            state_ssm[l].reshape(nseq, SSD_WIDTH, SSD_STATE), xr[0], gr[0],
            state_rg_conv[l].transpose(1, 0, 2), state_rg[l],
            scw, scb, dtb, alog, dvec, snw, rcw, rcb, wax, bax, lam, rnw)
        bias8 = jnp.broadcast_to(jnp.pad(sb_bias[l], (0, SUBLANES - SB_HEADS))[:, None], (SUBLANES, PAGE))
        pt_flat = (page_table + l * n_pool).reshape(-1).astype(jnp.int32)
        ysb = _sample_attn(pt_flat, qb.astype(F32).reshape(nseq, 1, SB_WIDTH), bias8, ck, cv,
                           nseq=nseq, n_pages=n_pages, npg=PAGES_PER_STEP)
        xs2, fconv_s = _sample_ffn(xs[0], yssd, ysb.reshape(nseq, SB_WIDTH), yrg,
                                   state_ffn_conv[l].transpose(1, 0, 2), sbn, wo, n2, wup, fcw, fcb, wd)
        xs = xs2[None]
        outs_s.append((kf.reshape(nseq, 1, SB_HEADS, HEAD_DIM),
                       vf.reshape(nseq, 1, SB_HEADS, HEAD_DIM),
                       ssm_s.reshape(nseq, SSD_HEADS, HEAD_DIM, SSD_STATE),
                       sconv_s.transpose(1, 0, 2),
                       rgh_s,
                       rconv_s.transpose(1, 0, 2),
                       fconv_s.transpose(1, 0, 2)))

    def stk(outs, i):
        return jnp.stack([o[i] for o in outs], axis=0)

    y_prompt = xp[:, p0 + N_META:]
    y_sample = xs.reshape(nseq, 1, D_MODEL)
    return (y_prompt, y_sample,
            stk(outs_p, 0), stk(outs_p, 1), stk(outs_p, 2), stk(outs_p, 3), stk(outs_p, 4), stk(outs_p, 5), stk(outs_p, 6),
            stk(outs_s, 0), stk(outs_s, 1), stk(outs_s, 2), stk(outs_s, 3), stk(outs_s, 4), stk(outs_s, 5), stk(outs_s, 6))
```

```python
import functools

import jax
import jax.numpy as jnp
import numpy as np
from jax import lax
from jax.experimental import pallas as pl
from jax.experimental.pallas import tpu as pltpu

F32 = jnp.float32
BF16 = jnp.bfloat16

D_MODEL = 1024
N_META = 16
HEAD_DIM = 64
SSD_WIDTH = 384
SSD_HEADS = 6
SSD_STATE = 128
SSD_XBC = 896
SB_WIDTH = 384
SB_HEADS = 6
RG_WIDTH = 256
RG_C = 8.0
D_FF = 2816
EPS = 1e-6

LANES = 128
SUBLANES = 8
PAIR = 2 * HEAD_DIM
DT_COLS = LANES
IN_COLS_PAD = 2 * SSD_WIDTH + 512 + 3 * SB_WIDTH + 2 * RG_WIDTH + DT_COLS
NEG_BIG = -1e30
LOG2E = 1.4426950408889634

ROW_TILE = 256
CHUNK = 128
ATT_TILE = 256
PAGE = 128
PAGES_PER_STEP = 8
VMEM_LIMIT = 56 * 1024 * 1024

NT_DIMS = (((1,), (1,)), ((), ()))


def _dot(a, b):
    return jnp.dot(a, b, preferred_element_type=F32)


def _dot_nt(a, b):
    return lax.dot_general(a, b, NT_DIMS, preferred_element_type=F32)


def _split_bf16(x, n):
    parts = []
    r = x
    for _ in range(n):
        p = r.astype(BF16)
        parts.append(p)
        r = r - p.astype(F32)
    return parts


def _dot_split_lhs(x, m, n):
    out = None
    for p in _split_bf16(x, n):
        t = _dot(p, m)
        out = t if out is None else out + t
    return out


def _dot_split_rhs(m, x, n):
    out = None
    for p in _split_bf16(x, n):
        t = _dot(m, p)
        out = t if out is None else out + t
    return out


def _rms(x, g):
    return x * lax.rsqrt(jnp.mean(x * x, axis=-1, keepdims=True) + EPS) * g


def _softplus(x):
    return jnp.maximum(x, 0.0) + jnp.log(1.0 + jnp.exp(-jnp.abs(x)))


def _silu(x):
    return x * jax.nn.sigmoid(x)


def _gelu_tanh(x):
    return 0.5 * x * (1.0 + jnp.tanh(0.7978845608028654 * (x + 0.044715 * (x * x * x))))


def _iota(shape, dim):
    return lax.broadcasted_iota(jnp.int32, shape, dim)


def _inproj_kernel(x_ref, g_ref, w_ref, bd_ref, qn_ref, kn_ref,
                   z_ref, xbc_ref, dt_ref, qb_ref, kb_ref, vb_ref, k_ref, v_ref, xr_ref, gr_ref,
                   *, tm, p0):
    x = x_ref[0]
    h = _rms(x, g_ref[...])
    if p0 > 0:
        rows = pl.program_id(1) * tm + _iota((tm, 1), 0)
        h = jnp.where(rows >= p0, h, 0.0)
    proj = _dot(h.astype(BF16), w_ref[...])
    z_ref[0] = proj[:, 0:384]
    xbc_ref[0] = proj[:, 384:1280]
    q = proj[:, 1280:1664]
    k = proj[:, 1664:2048]
    v = proj[:, 2048:2432]
    xr_ref[0] = proj[:, 2432:2688]
    gr_ref[0] = proj[:, 2688:2944]
    dt_ref[0] = proj[:, 2944:3072]
    bd = bd_ref[...]
    q_ms = _dot_split_lhs(q * q, bd, 2) * (1.0 / HEAD_DIM)
    k_ms = _dot_split_lhs(k * k, bd, 2) * (1.0 / HEAD_DIM)
    qn = q * lax.rsqrt(q_ms + EPS) * qn_ref[...]
    kn = k * lax.rsqrt(k_ms + EPS) * kn_ref[...]
    qb_ref[0] = (qn * (-LOG2E * HEAD_DIM ** -0.5)).astype(BF16)
    kb_ref[0] = kn.astype(BF16)
    vb_ref[0] = v.astype(BF16)
    k_ref[0] = kn
    v_ref[0] = v


def _inproj(x, g, w, bd, qn, kn, *, tm, p0):
    nb, rows, _ = x.shape
    grid = (nb, rows // tm)

    def tile(width):
        return pl.BlockSpec((1, tm, width), lambda b, i: (b, i, 0))

    def const(shape):
        return pl.BlockSpec(shape, lambda b, i: (0,) * len(shape), pipeline_mode=pl.Buffered(1))

    widths = (384, 896, DT_COLS, 384, 384, 384, 384, 384, 256, 256)
    dtypes = (F32, F32, F32, BF16, BF16, BF16, F32, F32, F32, F32)
    return pl.pallas_call(
        functools.partial(_inproj_kernel, tm=tm, p0=p0),
        grid=grid,
        in_specs=[tile(D_MODEL), const((1, D_MODEL)), const((D_MODEL, IN_COLS_PAD)),
                  const((384, 384)), const((1, 384)), const((1, 384))],
        out_specs=[tile(wd) for wd in widths],
        out_shape=[jax.ShapeDtypeStruct((nb, rows, wd), dt) for wd, dt in zip(widths, dtypes)],
        compiler_params=pltpu.CompilerParams(
            dimension_semantics=("parallel", "parallel"), vmem_limit_bytes=VMEM_LIMIT),
        name="inproj",
    )(x, g, w, bd, qn, kn)


def _ssd_kernel(xbc_ref, z_ref, dt_ref, cw_ref, cb_ref, dtb_ref, alog_ref, dvec_ref, nw_ref,
                y_ref, st_ref, tail_ref, xbuf, hst, *, c, p0):
    ci = pl.program_id(1)

    @pl.when(ci == 0)
    def _():
        xbuf[0:SUBLANES, :] = jnp.zeros((SUBLANES, SSD_XBC), F32)
        hst[...] = jnp.zeros_like(hst)

    xbuf[SUBLANES:SUBLANES + c, :] = xbc_ref[0]
    conv = cb_ref[...]
    for j in range(4):
        conv = conv + xbuf[5 + j:5 + j + c, :] * cw_ref[j:j + 1, :]
    last = xbuf[c:c + SUBLANES, :]
    tail_ref[0] = last
    xbuf[0:SUBLANES, :] = last
    act = _silu(conv)
    xs = act[:, 0:384]
    bm = act[:, 384:640]
    cm = act[:, 640:896]

    rows = ci * c + _iota((c, 1), 0)
    dt = _softplus(dt_ref[0] + dtb_ref[...])
    dt = jnp.where(rows >= p0, dt, 0.0)
    a = dt * (-jnp.exp(alog_ref[...]))
    r_i = _iota((c, c), 0)
    c_i = _iota((c, c), 1)
    tri = r_i >= c_i
    cum = _dot_split_rhs(tri.astype(BF16), a, 3)
    cum_t = cum.T
    cum_last = cum[c - 1:c, :]
    e_cum = jnp.exp(cum)
    dec = jnp.exp(cum_last - cum) * dt
    e_last = jnp.exp(cum_last)

    lane = _iota((1, LANES), 1)
    lo_half = lane < HEAD_DIM
    rowp = _iota((LANES, 1), 0)
    top_half = rowp < HEAD_DIM
    cbs = [_dot_nt(cm[:, g * LANES:(g + 1) * LANES], bm[:, g * LANES:(g + 1) * LANES]) for g in range(2)]

    def col(m, h):
        return m[:, h:h + 1]

    ys = []
    for pr in range(3):
        ha, hb = 2 * pr, 2 * pr + 1
        ga, gb = ha // 3, hb // 3
        xp = xs[:, pr * LANES:(pr + 1) * LANES]
        xd = xp * jnp.where(lo_half, col(dt, ha), col(dt, hb))
        hpair = hst[pr * LANES:(pr + 1) * LANES, :]
        y_intra = []
        for h, g in ((ha, ga), (hb, gb)):
            seg = col(cum, h) - cum_t[h:h + 1, :]
            lm = jnp.exp(jnp.where(tri, seg, -jnp.inf))
            y_intra.append(_dot(cbs[g] * lm, xd))
        yi_a = _dot_nt(cm[:, ga * LANES:(ga + 1) * LANES], hpair)
        yi_b = yi_a if gb == ga else _dot_nt(cm[:, gb * LANES:(gb + 1) * LANES], hpair)
        y = jnp.where(lo_half, y_intra[0] + yi_a * col(e_cum, ha), y_intra[1] + yi_b * col(e_cum, hb))
        ys.append(y + xp * dvec_ref[:, pr * LANES:(pr + 1) * LANES])
        xe_t = (xp * jnp.where(lo_half, col(dec, ha), col(dec, hb))).T
        hn_a = _dot(xe_t, bm[:, ga * LANES:(ga + 1) * LANES])
        hn_b = hn_a if gb == ga else _dot(xe_t, bm[:, gb * LANES:(gb + 1) * LANES])
        keep = jnp.where(top_half, col(e_last, ha), col(e_last, hb))
        hst[pr * LANES:(pr + 1) * LANES, :] = hpair * keep + jnp.where(top_half, hn_a, hn_b)

    y = jnp.concatenate(ys, axis=-1) * _silu(z_ref[0])
    y_ref[0] = _rms(y, nw_ref[...]).astype(BF16)

    @pl.when(ci == pl.num_programs(1) - 1)
    def _():
        st_ref[0] = hst[...]


def _ssd(xbc, z, dt, cw, cb, dtb, alog, dvec, nw, *, c, p0):
    nb, rows, _ = xbc.shape

    def tile(width):
        return pl.BlockSpec((1, c, width), lambda b, i: (b, i, 0))

    def const(shape):
        return pl.BlockSpec(shape, lambda b, i: (0,) * len(shape))

    return pl.pallas_call(
        functools.partial(_ssd_kernel, c=c, p0=p0),
        grid=(nb, rows // c),
        in_specs=[tile(SSD_XBC), tile(SSD_WIDTH), tile(DT_COLS), const((4, SSD_XBC)), const((1, SSD_XBC)),
                  const((1, DT_COLS)), const((1, DT_COLS)), const((1, SSD_WIDTH)), const((1, SSD_WIDTH))],
        out_specs=[tile(SSD_WIDTH),
                   pl.BlockSpec((1, SSD_WIDTH, SSD_STATE), lambda b, i: (b, 0, 0)),
                   pl.BlockSpec((1, SUBLANES, SSD_XBC), lambda b, i: (b, 0, 0))],
        out_shape=[jax.ShapeDtypeStruct((nb, rows, SSD_WIDTH), BF16),
                   jax.ShapeDtypeStruct((nb, SSD_WIDTH, SSD_STATE), F32),
                   jax.ShapeDtypeStruct((nb, SUBLANES, SSD_XBC), F32)],
        scratch_shapes=[pltpu.VMEM((SUBLANES + c, SSD_XBC), F32), pltpu.VMEM((SSD_WIDTH, SSD_STATE), F32)],
        compiler_params=pltpu.CompilerParams(
            dimension_semantics=("arbitrary", "arbitrary"), vmem_limit_bytes=VMEM_LIMIT),
        name="ssd_scan",
    )(xbc, z, dt, cw, cb, dtb, alog, dvec, nw)


def _rg_gates(xc, wax_ref, bax_ref, lam_ref):
    ri = jax.nn.sigmoid(_dot(xc, wax_ref[...]) + bax_ref[...])
    r = ri[:, 0:RG_WIDTH]
    i = ri[:, RG_WIDTH:2 * RG_WIDTH]
    log_a = RG_C * r * (-_softplus(-lam_ref[...]))
    a = jnp.exp(log_a)
    u = jnp.sqrt(1.0 - jnp.exp(2.0 * log_a)) * (i * xc)
    return a, u


def _rg_kernel(xr_ref, gr_ref, cw_ref, cb_ref, wax_ref, bax_ref, lam_ref, nw_ref,
               y_ref, h_ref, tail_ref, xbuf, hprev, *, c, p0):
    ci = pl.program_id(1)

    @pl.when(ci == 0)
    def _():
        xbuf[0:SUBLANES, :] = jnp.zeros((SUBLANES, RG_WIDTH), F32)
        hprev[...] = jnp.zeros_like(hprev)

    xbuf[SUBLANES:SUBLANES + c, :] = xr_ref[0]
    xc = cb_ref[...]
    for j in range(4):
        xc = xc + xbuf[5 + j:5 + j + c, :] * cw_ref[j:j + 1, :]
    last = xbuf[c:c + SUBLANES, :]
    tail_ref[0] = last
    xbuf[0:SUBLANES, :] = last

    a, u = _rg_gates(xc, wax_ref, bax_ref, lam_ref)
    rowi = _iota((c, 1), 0)
    valid = (ci * c + rowi) >= p0
    a = jnp.where(valid, a, 1.0)
    u = jnp.where(valid, u, 0.0)
    d = 1
    while d < c:
        a_sh = jnp.where(rowi >= d, pltpu.roll(a, d, 0), 1.0)
        u_sh = jnp.where(rowi >= d, pltpu.roll(u, d, 0), 0.0)
        u = a * u_sh + u
        a = a * a_sh
        d *= 2
    h = a * hprev[...] + u
    h_last = h[c - 1:c, :]
    hprev[...] = h_last
    h_ref[0] = h_last
    y = h * _gelu_tanh(gr_ref[0])
    y_ref[0] = _rms(y, nw_ref[...]).astype(BF16)


def _rg(xr, gr, cw, cb, wax, bax, lam, nw, *, c, p0):
    nb, rows, _ = xr.shape

    def tile():
        return pl.BlockSpec((1, c, RG_WIDTH), lambda b, i: (b, i, 0))

    def const(shape):
        return pl.BlockSpec(shape, lambda b, i: (0,) * len(shape))

    return pl.pallas_call(
        functools.partial(_rg_kernel, c=c, p0=p0),
        grid=(nb, rows // c),
        in_specs=[tile(), tile(), const((4, RG_WIDTH)), const((1, RG_WIDTH)), const((RG_WIDTH, 2 * RG_WIDTH)),
                  const((1, 2 * RG_WIDTH)), const((1, RG_WIDTH)), const((1, RG_WIDTH))],
        out_specs=[tile(),
                   pl.BlockSpec((1, 1, RG_WIDTH), lambda b, i: (b, 0, 0)),
                   pl.BlockSpec((1, SUBLANES, RG_WIDTH), lambda b, i: (b, 0, 0))],
        out_shape=[jax.ShapeDtypeStruct((nb, rows, RG_WIDTH), BF16),
                   jax.ShapeDtypeStruct((nb, 1, RG_WIDTH), F32),
                   jax.ShapeDtypeStruct((nb, SUBLANES, RG_WIDTH), F32)],
        scratch_shapes=[pltpu.VMEM((SUBLANES + c, RG_WIDTH), F32), pltpu.VMEM((1, RG_WIDTH), F32)],
        compiler_params=pltpu.CompilerParams(dimension_semantics=("arbitrary", "arbitrary")),
        name="rglru_scan",
    )(xr, gr, cw, cb, wax, bax, lam, nw)


def _log2_one_minus_beta(zn):
    neg_abs = pltpu.bitcast(pltpu.bitcast(zn, jnp.uint32) | jnp.uint32(0x80000000), F32)
    return jnp.minimum(zn, 0.0) - jnp.log2(1.0 + jnp.exp2(neg_abs))


def _attn_kernel(q_ref, k_ref, v_ref, kb_ref, o_ref, zbuf, lbuf, sbuf, tbuf, cbuf, acc, *, t):
    qi = pl.program_id(2)
    n = qi + 1
    q = q_ref[0]
    lane = _iota((1, PAIR), 1)
    lo_half = lane < HEAD_DIM
    zero = jnp.zeros((), BF16)
    qs = (jnp.where(lo_half, q, zero), jnp.where(lo_half, zero, q))
    r_i = _iota((t, t), 0)
    c_i = _iota((t, t), 1)
    incl_mat = (r_i >= c_i).astype(BF16)

    cbuf[...] = jnp.zeros_like(cbuf)
    acc[...] = jnp.zeros_like(acc)

    def stage_a(j, diag):
        ks = k_ref[0, pl.ds(pl.multiple_of(j * t, t), t), :]
        kbias = kb_ref[0, j]
        for hh in range(2):
            zn = _dot_nt(qs[hh], ks) + kbias[hh:hh + 1, :]
            if diag:
                zn = zn + jnp.where(c_i < r_i, 0.0, -NEG_BIG)
            zbuf[hh] = zn
            lbuf[hh] = _log2_one_minus_beta(zn).astype(BF16)

    def stage_b():
        for hh in range(2):
            incl = _dot(lbuf[hh], incl_mat)
            sbuf[hh] = incl - zbuf[hh]
            tbuf[hh] = incl[:, 0:1]

    def stage_c(j):
        vs = v_ref[0, pl.ds(pl.multiple_of(j * t, t), t), :]
        vh = (jnp.where(lo_half, vs, zero), jnp.where(lo_half, zero, vs))
        a = acc[...]
        for hh in range(2):
            c = cbuf[hh]
            w = jnp.exp2(sbuf[hh] + c).astype(BF16)
            a = a + _dot(w, vh[hh])
            cbuf[hh] = c + tbuf[hh]
        acc[...] = a

    stage_a(qi, True)
    stage_b()

    @pl.when(n > 1)
    def _():
        stage_a(qi - 1, False)

    def body(i, carry):
        j = qi - i
        stage_c(j + 2)
        stage_b()
        stage_a(j, False)
        return carry

    lax.fori_loop(2, n, body, 0)

    @pl.when(n > 1)
    def _():
        stage_c(1)
        stage_b()

    stage_c(0)
    o_ref[0] = acc[...]


def _attn(qb, kb, vb, kbias, *, t):
    nb, rows, _ = qb.shape
    nk = rows // t
    return pl.pallas_call(
        functools.partial(_attn_kernel, t=t),
        grid=(nb, SB_WIDTH // PAIR, nk),
        in_specs=[pl.BlockSpec((1, t, PAIR), lambda b, p, i: (b, i, p)),
                  pl.BlockSpec((1, rows, PAIR), lambda b, p, i: (b, 0, p)),
                  pl.BlockSpec((1, rows, PAIR), lambda b, p, i: (b, 0, p)),
                  pl.BlockSpec((1, nk, SUBLANES, t), lambda b, p, i: (p, 0, 0, 0))],
        out_specs=pl.BlockSpec((1, t, PAIR), lambda b, p, i: (b, i, p)),
        out_shape=jax.ShapeDtypeStruct((nb, rows, SB_WIDTH), F32),
        scratch_shapes=[pltpu.VMEM((2, t, t), F32), pltpu.VMEM((2, t, t), BF16), pltpu.VMEM((2, t, t), F32),
                        pltpu.VMEM((2, t, 1), F32), pltpu.VMEM((2, t, 1), F32), pltpu.VMEM((t, PAIR), F32)],
        compiler_params=pltpu.CompilerParams(
            dimension_semantics=("arbitrary", "arbitrary", "arbitrary"), vmem_limit_bytes=VMEM_LIMIT),
        name="sb_attention",
    )(qb, kb, vb, kbias)


def _ffn_kernel(x_ref, yssd_ref, ysb_ref, yrg_ref, sbn_ref, wo_ref, n2_ref, wup_ref, cw_ref, cb_ref, wd_ref,
                out_ref, tail_ref, gbuf, *, tm, p0):
    i = pl.program_id(1)

    @pl.when(i == 0)
    def _():
        gbuf[0:SUBLANES, :] = jnp.zeros((SUBLANES, D_FF), F32)

    ysb = _rms(ysb_ref[0], sbn_ref[...]).astype(BF16)
    mix = jnp.concatenate([yssd_ref[0], ysb, yrg_ref[0]], axis=-1)
    xm = x_ref[0] + _dot(mix, wo_ref[...])
    h2 = _rms(xm, n2_ref[...])
    if p0 > 0:
        rows = i * tm + _iota((tm, 1), 0)
        h2 = jnp.where(rows >= p0, h2, 0.0)
    gu = _dot(h2.astype(BF16), wup_ref[...])
    gbuf[SUBLANES:SUBLANES + tm, :] = gu[:, 0:D_FF]
    conv = cb_ref[...]
    for j in range(3):
        conv = conv + gbuf[6 + j:6 + j + tm, :] * cw_ref[j:j + 1, :]
    last = gbuf[tm:tm + SUBLANES, :]
    tail_ref[0] = last
    gbuf[0:SUBLANES, :] = last
    act = (_silu(conv) * gu[:, D_FF:2 * D_FF]).astype(BF16)
    out_ref[0] = xm + _dot(act, wd_ref[...])


def _ffn(x, yssd, ysb, yrg, sbn, wo, n2, wup, cw, cb, wd, *, tm, p0):
    nb, rows, _ = x.shape

    def tile(width):
        return pl.BlockSpec((1, tm, width), lambda b, i: (b, i, 0))

    def const(shape):
        return pl.BlockSpec(shape, lambda b, i: (0,) * len(shape), pipeline_mode=pl.Buffered(1))

    return pl.pallas_call(
        functools.partial(_ffn_kernel, tm=tm, p0=p0),
        grid=(nb, rows // tm),
        in_specs=[tile(D_MODEL), tile(SSD_WIDTH), tile(SB_WIDTH), tile(RG_WIDTH), const((1, SB_WIDTH)),
                  const((D_MODEL, D_MODEL)), const((1, D_MODEL)), const((D_MODEL, 2 * D_FF)),
                  const((3, D_FF)), const((1, D_FF)), const((D_FF, D_MODEL))],
        out_specs=[tile(D_MODEL), pl.BlockSpec((1, SUBLANES, D_FF), lambda b, i: (b, 0, 0))],
        out_shape=[jax.ShapeDtypeStruct((nb, rows, D_MODEL), F32),
                   jax.ShapeDtypeStruct((nb, SUBLANES, D_FF), F32)],
        scratch_shapes=[pltpu.VMEM((SUBLANES + tm, D_FF), F32)],
        compiler_params=pltpu.CompilerParams(
            dimension_semantics=("arbitrary", "arbitrary"), vmem_limit_bytes=VMEM_LIMIT),
        name="outproj_ffn",
    )(x, yssd, ysb, yrg, sbn, wo, n2, wup, cw, cb, wd)


def _sample_mix_kernel(z_ref, xbc_ref, dt_ref, sbuf_ref, st_ref, xr_ref, gr_ref, rbuf_ref, rh_ref,
                       scw_ref, scb_ref, dtb_ref, alog_ref, dvec_ref, snw_ref,
                       rcw_ref, rcb_ref, wax_ref, bax_ref, lam_ref, rnw_ref,
                       yssd_ref, st_out_ref, sbuf_out_ref, yrg_ref, rh_out_ref, rbuf_out_ref,
                       yscr, *, nseq):
    xbc = xbc_ref[...]
    conv = scb_ref[...] + xbc * scw_ref[3:4, :]
    for j in range(3):
        conv = conv + sbuf_ref[j] * scw_ref[j:j + 1, :]
    sbuf_out_ref[0] = sbuf_ref[1]
    sbuf_out_ref[1] = sbuf_ref[2]
    sbuf_out_ref[2] = xbc
    act = _silu(conv)
    xs = act[:, 0:384]
    bm = act[:, 384:640]
    cm = act[:, 640:896]
    dt = _softplus(dt_ref[...] + dtb_ref[...])
    da = jnp.exp(dt * (-jnp.exp(alog_ref[...])))

    lane = _iota((1, LANES), 1)
    lo_half = lane < HEAD_DIM
    rowp = _iota((LANES, 1), 0)
    top_half = rowp < HEAD_DIM
    pad_rows = LANES - nseq
    yscr[...] = jnp.zeros_like(yscr)
    for pr in range(3):
        ha, hb = 2 * pr, 2 * pr + 1
        ga, gb = ha // 3, hb // 3
        xp = xs[:, pr * LANES:(pr + 1) * LANES]
        xd = xp * jnp.where(lo_half, dt[:, ha:ha + 1], dt[:, hb:hb + 1])
        xd_t = jnp.concatenate([xd, jnp.zeros((pad_rows, LANES), F32)], axis=0).T
        for b in range(nseq):
            hpair = st_ref[b, pr * LANES:(pr + 1) * LANES, :]
            keep = jnp.where(top_half, da[b:b + 1, ha:ha + 1], da[b:b + 1, hb:hb + 1])
            brow = jnp.where(top_half, bm[b:b + 1, ga * LANES:(ga + 1) * LANES],
                             bm[b:b + 1, gb * LANES:(gb + 1) * LANES])
            hnew = hpair * keep + xd_t[:, b:b + 1] * brow
            st_out_ref[b, pr * LANES:(pr + 1) * LANES, :] = hnew
            crow = jnp.where(top_half, cm[b:b + 1, ga * LANES:(ga + 1) * LANES],
                             cm[b:b + 1, gb * LANES:(gb + 1) * LANES])
            ycol = jnp.sum(hnew * crow, axis=1, keepdims=True)
            yscr[pr, :, b:b + 1] = ycol
    ys = []
    for pr in range(3):
        ys.append(yscr[pr].T[0:nseq, :])
    y = jnp.concatenate(ys, axis=-1) + xs * dvec_ref[...]
    y = y * _silu(z_ref[...])
    yssd_ref[...] = _rms(y, snw_ref[...]).astype(BF16)

    xr = xr_ref[...]
    xc = rcb_ref[...] + xr * rcw_ref[3:4, :]
    for j in range(3):
        xc = xc + rbuf_ref[j] * rcw_ref[j:j + 1, :]
    rbuf_out_ref[0] = rbuf_ref[1]
    rbuf_out_ref[1] = rbuf_ref[2]
    rbuf_out_ref[2] = xr
    a, u = _rg_gates(xc, wax_ref, bax_ref, lam_ref)
    h = a * rh_ref[...] + u
    rh_out_ref[...] = h
    yrg_ref[...] = _rms(h * _gelu_tanh(gr_ref[...]), rnw_ref[...]).astype(BF16)


def _sample_mix(z, xbc, dt, sbuf, st, xr, gr, rbuf, rh, scw, scb, dtb, alog, dvec, snw,
                rcw, rcb, wax, bax, lam, rnw):
    nseq = z.shape[0]
    return pl.pallas_call(
        functools.partial(_sample_mix_kernel, nseq=nseq),
        out_shape=[jax.ShapeDtypeStruct((nseq, SSD_WIDTH), BF16),
                   jax.ShapeDtypeStruct((nseq, SSD_WIDTH, SSD_STATE), F32),
                   jax.ShapeDtypeStruct((3, nseq, SSD_XBC), F32),
                   jax.ShapeDtypeStruct((nseq, RG_WIDTH), BF16),
                   jax.ShapeDtypeStruct((nseq, RG_WIDTH), F32),
                   jax.ShapeDtypeStruct((3, nseq, RG_WIDTH), F32)],
        scratch_shapes=[pltpu.VMEM((3, LANES, LANES), F32)],
        compiler_params=pltpu.CompilerParams(vmem_limit_bytes=VMEM_LIMIT),
        name="sample_mixers",
    )(z, xbc, dt, sbuf, st, xr, gr, rbuf, rh, scw, scb, dtb, alog, dvec, snw, rcw, rcb, wax, bax, lam, rnw)


def _sample_attn_kernel(pt_ref, q_ref, bias_ref, *refs, npg):
    del pt_ref
    k_refs = refs[0:npg]
    v_refs = refs[npg:2 * npg]
    o_ref = refs[2 * npg]
    carry_ref, acc_ref = refs[2 * npg + 1:]
    g = pl.program_id(1)

    @pl.when(g == 0)
    def _():
        carry_ref[...] = jnp.zeros_like(carry_ref)
        acc_ref[...] = jnp.zeros_like(acc_ref)

    hrow = _iota((SUBLANES, SB_WIDTH), 0)
    hlane = _iota((SUBLANES, SB_WIDTH), 1) // HEAD_DIM
    own = hrow == hlane
    qbd = jnp.where(own, jnp.broadcast_to(q_ref[0], (SUBLANES, SB_WIDTH)), 0.0).astype(BF16)
    r_i = _iota((PAGE, PAGE), 0)
    c_i = _iota((PAGE, PAGE), 1)
    incl_mat = (r_i >= c_i).astype(BF16)
    carry = carry_ref[...]
    acc = acc_ref[...]
    zs, ls = [], []
    for r in range(npg):
        zn = _dot_nt(qbd, k_refs[r][0]) + bias_ref[...]
        zs.append(zn)
        ls.append(_log2_one_minus_beta(zn).astype(BF16))
    for r in range(npg):
        incl = _dot(ls[r], incl_mat)
        w = jnp.exp2(incl - zs[r] + carry).astype(BF16)
        acc = acc + _dot(w, v_refs[r][0])
        carry = carry + incl[:, 0:1]
    carry_ref[...] = carry
    acc_ref[...] = acc

    @pl.when(g == pl.num_programs(1) - 1)
    def _():
        o_ref[0] = jnp.sum(jnp.where(own, acc, 0.0), axis=0, keepdims=True)


def _sample_attn(pt_flat, q3, bias8, ck, cv, *, nseq, n_pages, npg):
    steps = n_pages // npg

    def page_spec(r):
        def imap(b, g, pt):
            return (pt[b * n_pages + (n_pages - 1 - (g * npg + r))], 0, 0)
        return pl.BlockSpec((1, PAGE, SB_WIDTH), imap)

    grid_spec = pltpu.PrefetchScalarGridSpec(
        num_scalar_prefetch=1,
        grid=(nseq, steps),
        in_specs=[pl.BlockSpec((1, 1, SB_WIDTH), lambda b, g, pt: (b, 0, 0)),
                  pl.BlockSpec((SUBLANES, PAGE), lambda b, g, pt: (0, 0))]
                 + [page_spec(r) for r in range(npg)] + [page_spec(r) for r in range(npg)],
        out_specs=pl.BlockSpec((1, 1, SB_WIDTH), lambda b, g, pt: (b, 0, 0)),
        scratch_shapes=[pltpu.VMEM((SUBLANES, 1), F32), pltpu.VMEM((SUBLANES, SB_WIDTH), F32)],
    )
    return pl.pallas_call(
        functools.partial(_sample_attn_kernel, npg=npg),
        grid_spec=grid_spec,
        out_shape=jax.ShapeDtypeStruct((nseq, 1, SB_WIDTH), F32),
        compiler_params=pltpu.CompilerParams(dimension_semantics=("arbitrary", "arbitrary")),
        name="sample_attention",
    )(pt_flat, q3, bias8, *([ck] * npg), *([cv] * npg))


def _sample_ffn_kernel(x_ref, yssd_ref, ysb_ref, yrg_ref, fbuf_ref, sbn_ref, wo_ref, n2_ref, wup_ref,
                       cw_ref, cb_ref, wd_ref, out_ref, fbuf_out_ref):
    ysb = _rms(ysb_ref[...], sbn_ref[...]).astype(BF16)
    mix = jnp.concatenate([yssd_ref[...], ysb, yrg_ref[...]], axis=-1)
    xm = x_ref[...] + _dot(mix, wo_ref[...])
    h2 = _rms(xm, n2_ref[...])
    gu = _dot(h2.astype(BF16), wup_ref[...])
    g = gu[:, 0:D_FF]
    conv = cb_ref[...] + fbuf_ref[0] * cw_ref[0:1, :] + fbuf_ref[1] * cw_ref[1:2, :] + g * cw_ref[2:3, :]
    fbuf_out_ref[0] = fbuf_ref[1]
    fbuf_out_ref[1] = g
    act = (_silu(conv) * gu[:, D_FF:2 * D_FF]).astype(BF16)
    out_ref[...] = xm + _dot(act, wd_ref[...])


def _sample_ffn(x, yssd, ysb, yrg, fbuf, sbn, wo, n2, wup, cw, cb, wd):
    nseq = x.shape[0]
    return pl.pallas_call(
        _sample_ffn_kernel,
        out_shape=[jax.ShapeDtypeStruct((nseq, D_MODEL), F32),
                   jax.ShapeDtypeStruct((2, nseq, D_FF), F32)],
        compiler_params=pltpu.CompilerParams(vmem_limit_bytes=VMEM_LIMIT),
        name="sample_outproj_ffn",
    )(x, yssd, ysb, yrg, fbuf, sbn, wo, n2, wup, cw, cb, wd)


def _pad_cols(a, width):
    return jnp.pad(a, ((0, 0), (0, width - a.shape[-1])))


def _block_diag(w):
    nblk, bi, bj = w.shape
    out = jnp.zeros((nblk * bi, nblk * bj), w.dtype)
    for h in range(nblk):
        out = out.at[h * bi:(h + 1) * bi, h * bj:(h + 1) * bj].set(w[h])
    return out


def kernel(x_prompt, x_sample, cache_k, cache_v, page_table, state_ssm, state_ssm_conv, state_rg, state_rg_conv, state_ffn_conv, meta_tokens, norm1, w_in, ssd_conv_w, ssd_conv_b, ssd_dt_bias, ssd_a_log, ssd_d, ssd_norm, q_norm, k_norm, sb_bias, sb_out_norm, rg_conv_w, rg_conv_b, rg_wa, rg_ba, rg_wx, rg_bx, rg_lambda, rg_out_norm, w_out, norm2, w_up, ffn_conv_w, ffn_conv_b, w_down):
    depth = w_in.shape[0]
    nb, seq, _ = x_prompt.shape
    nseq = x_sample.shape[0]
    n_pool = cache_k.shape[1]
    n_pages = page_table.shape[1]
    t_real = N_META + seq
    t_pad = -(-t_real // ATT_TILE) * ATT_TILE
    p0 = t_pad - t_real
    nk = t_pad // ATT_TILE

    meta = jnp.broadcast_to(meta_tokens[None], (nb, N_META, D_MODEL))
    xp = jnp.concatenate([jnp.zeros((nb, p0, D_MODEL), F32), meta, x_prompt], axis=1)
    xs = x_sample.reshape(1, nseq, D_MODEL)

    head_of = np.arange(SB_WIDTH) // HEAD_DIM
    bd = jnp.asarray(head_of[:, None] == head_of[None, :], BF16)
    key_valid = jnp.arange(t_pad) >= p0
    ck = cache_k.reshape(depth * n_pool, PAGE, SB_WIDTH).astype(BF16)
    cv = cache_v.reshape(depth * n_pool, PAGE, SB_WIDTH).astype(BF16)

    outs_p, outs_s = [], []
    for l in range(depth):
        wz, wxbc, wdt, wq, wk, wv, wxr, wgr = jnp.split(
            w_in[l], np.cumsum([384, 896, 6, 384, 384, 384, 256])[:].tolist(), axis=1)
        w1 = jnp.concatenate([wz, wxbc, wq, wk, wv, wxr, wgr, _pad_cols(wdt, DT_COLS)], axis=1).astype(BF16)
        g1 = norm1[l][None]
        qn = jnp.tile(q_norm[l], SB_HEADS)[None]
        kn = jnp.tile(k_norm[l], SB_HEADS)[None]
        dtb = _pad_cols(ssd_dt_bias[l][None], DT_COLS)
        alog = _pad_cols(ssd_a_log[l][None], DT_COLS)
        dvec = jnp.repeat(ssd_d[l], HEAD_DIM)[None]
        snw = ssd_norm[l][None]
        scw, scb = ssd_conv_w[l], ssd_conv_b[l][None]
        rcw, rcb = rg_conv_w[l], rg_conv_b[l][None]
        wax = jnp.concatenate([_block_diag(rg_wa[l]), _block_diag(rg_wx[l])], axis=1)
        bax = jnp.concatenate([rg_ba[l].reshape(1, RG_WIDTH), rg_bx[l].reshape(1, RG_WIDTH)], axis=1)
        lam = rg_lambda[l][None]
        rnw = rg_out_norm[l][None]
        sbn = sb_out_norm[l][None]
        wo = w_out[l].astype(BF16)
        n2 = norm2[l][None]
        wup = w_up[l].astype(BF16)
        fcw, fcb = ffn_conv_w[l], ffn_conv_b[l][None]
        wd = w_down[l].astype(BF16)

        z, xbc, dt, qb, kb, vb, kf, vf, xr, gr = _inproj(xp, g1, w1, bd, qn, kn, tm=ROW_TILE, p0=p0)
        yssd, ssm_p, sconv_p = _ssd(xbc, z, dt, scw, scb, dtb, alog, dvec, snw, c=CHUNK, p0=p0)
        yrg, rgh_p, rconv_p = _rg(xr, gr, rcw, rcb, wax, bax, lam, rnw, c=CHUNK, p0=p0)
        neg_bias = -LOG2E * sb_bias[l]
        kbias = jnp.where(key_valid[None, :], neg_bias[:, None], -NEG_BIG)
        kbias = kbias.reshape(3, 2, nk, ATT_TILE).transpose(0, 2, 1, 3)
        kbias = jnp.pad(kbias, ((0, 0), (0, 0), (0, SUBLANES - 2), (0, 0)))
        ysb = _attn(qb, kb, vb, kbias, t=ATT_TILE)
        xp, fconv_p = _ffn(xp, yssd, ysb, yrg, sbn, wo, n2, wup, fcw, fcb, wd, tm=ROW_TILE, p0=p0)
        outs_p.append((kf[:, p0:].reshape(nb, t_real, SB_HEADS, HEAD_DIM),
                       vf[:, p0:].reshape(nb, t_real, SB_HEADS, HEAD_DIM),
                       ssm_p.reshape(nb, SSD_HEADS, HEAD_DIM, SSD_STATE),
                       sconv_p[:, SUBLANES - 3:],
                       rgh_p.reshape(nb, RG_WIDTH),
                       rconv_p[:, SUBLANES - 3:],
                       fconv_p[:, SUBLANES - 2:]))

        z, xbc, dt, qb, kb, vb, kf, vf, xr, gr = _inproj(xs, g1, w1, bd, qn, kn, tm=nseq, p0=0)
        yssd, ssm_s, sconv_s, yrg, rgh_s, rconv_s = _sample_mix(
            z[0], xbc[0], dt[0], state_ssm_conv[l].transpose(1, 0, 2),
            state_ssm[l].reshape(nseq, SSD_WIDTH, SSD_STATE), xr[0], gr[0],
            state_rg_conv[l].transpose(1, 0, 2), state_rg[l],
            scw, scb, dtb, alog, dvec, snw, rcw, rcb, wax, bax, lam, rnw)
        bias8 = jnp.broadcast_to(jnp.pad(neg_bias, (0, SUBLANES - SB_HEADS))[:, None], (SUBLANES, PAGE))
        pt_flat = (page_table + l * n_pool).reshape(-1).astype(jnp.int32)
        ysb = _sample_attn(pt_flat, qb.astype(F32).reshape(nseq, 1, SB_WIDTH), bias8, ck, cv,
                           nseq=nseq, n_pages=n_pages, npg=PAGES_PER_STEP)
        xs2, fconv_s = _sample_ffn(xs[0], yssd, ysb.reshape(nseq, SB_WIDTH), yrg,
                                   state_ffn_conv[l].transpose(1, 0, 2), sbn, wo, n2, wup, fcw, fcb, wd)
        xs = xs2[None]
        outs_s.append((kf.reshape(nseq, 1, SB_HEADS, HEAD_DIM),
                       vf.reshape(nseq, 1, SB_HEADS, HEAD_DIM),
                       ssm_s.reshape(nseq, SSD_HEADS, HEAD_DIM, SSD_STATE),
                       sconv_s.transpose(1, 0, 2),
                       rgh_s,
                       rconv_s.transpose(1, 0, 2),
                       fconv_s.transpose(1, 0, 2)))

    def stk(outs, i):
        return jnp.stack([o[i] for o in outs], axis=0)

    y_prompt = xp[:, p0 + N_META:]
    y_sample = xs.reshape(nseq, 1, D_MODEL)
    return (y_prompt, y_sample,
            stk(outs_p, 0), stk(outs_p, 1), stk(outs_p, 2), stk(outs_p, 3), stk(outs_p, 4), stk(outs_p, 5), stk(outs_p, 6),
            stk(outs_s, 0), stk(outs_s, 1), stk(outs_s, 2), stk(outs_s, 3), stk(outs_s, 4), stk(outs_s, 5), stk(outs_s, 6))
```

```python
import functools

import jax
import jax.numpy as jnp
import numpy as np
from jax import lax
from jax.experimental import pallas as pl
from jax.experimental.pallas import tpu as pltpu

F32 = jnp.float32
BF16 = jnp.bfloat16

D_MODEL = 1024
N_META = 16
HEAD_DIM = 64
SSD_WIDTH = 384
SSD_HEADS = 6
SSD_STATE = 128
SSD_XBC = 896
SB_WIDTH = 384
SB_HEADS = 6
RG_WIDTH = 256
RG_C = 8.0
D_FF = 2816
EPS = 1e-6

LANES = 128
SUBLANES = 8
PAIR = 2 * HEAD_DIM
DT_COLS = LANES
IN_COLS_PAD = 2 * SSD_WIDTH + 512 + 3 * SB_WIDTH + 2 * RG_WIDTH + DT_COLS
NEG_BIG = -1e30
LOG2E = 1.4426950408889634

ROW_TILE = 256
CHUNK = 128
ATT_TILE = 256
PAGE = 128
PAGES_PER_STEP = 8
VMEM_LIMIT = 56 * 1024 * 1024

NT_DIMS = (((1,), (1,)), ((), ()))


def _dot(a, b):
    return jnp.dot(a, b, preferred_element_type=F32)


def _dot_nt(a, b):
    return lax.dot_general(a, b, NT_DIMS, preferred_element_type=F32)


def _split_bf16(x, n):
    parts = []
    r = x
    for _ in range(n):
        p = r.astype(BF16)
        parts.append(p)
        r = r - p.astype(F32)
    return parts


def _dot_split_lhs(x, m, n):
    out = None
    for p in _split_bf16(x, n):
        t = _dot(p, m)
        out = t if out is None else out + t
    return out


def _dot_split_rhs(m, x, n):
    out = None
    for p in _split_bf16(x, n):
        t = _dot(m, p)
        out = t if out is None else out + t
    return out


def _rms(x, g):
    return x * lax.rsqrt(jnp.mean(x * x, axis=-1, keepdims=True) + EPS) * g


def _softplus(x):
    return jnp.maximum(x, 0.0) + jnp.log(1.0 + jnp.exp(-jnp.abs(x)))


def _silu(x):
    return x * jax.nn.sigmoid(x)


def _gelu_tanh(x):
    return 0.5 * x * (1.0 + jnp.tanh(0.7978845608028654 * (x + 0.044715 * (x * x * x))))


def _iota(shape, dim):
    return lax.broadcasted_iota(jnp.int32, shape, dim)


def _inproj_kernel(x_ref, g_ref, w_ref, bd_ref, qn_ref, kn_ref,
                   z_ref, xbc_ref, dt_ref, qb_ref, kb_ref, vb_ref, k_ref, v_ref, xr_ref, gr_ref,
                   *, tm, p0):
    x = x_ref[0]
    h = _rms(x, g_ref[...])
    if p0 > 0:
        rows = pl.program_id(1) * tm + _iota((tm, 1), 0)
        h = jnp.where(rows >= p0, h, 0.0)
    proj = _dot(h.astype(BF16), w_ref[...])
    z_ref[0] = proj[:, 0:384]
    xbc_ref[0] = proj[:, 384:1280]
    q = proj[:, 1280:1664]
    k = proj[:, 1664:2048]
    v = proj[:, 2048:2432]
    xr_ref[0] = proj[:, 2432:2688]
    gr_ref[0] = proj[:, 2688:2944]
    dt_ref[0] = proj[:, 2944:3072]
    bd = bd_ref[...]
    q_ms = _dot_split_lhs(q * q, bd, 2) * (1.0 / HEAD_DIM)
    k_ms = _dot_split_lhs(k * k, bd, 2) * (1.0 / HEAD_DIM)
    qn = q * lax.rsqrt(q_ms + EPS) * qn_ref[...]
    kn = k * lax.rsqrt(k_ms + EPS) * kn_ref[...]
    qb_ref[0] = (qn * (-LOG2E * HEAD_DIM ** -0.5)).astype(BF16)
    kb_ref[0] = kn.astype(BF16)
    vb_ref[0] = v.astype(BF16)
    k_ref[0] = kn
    v_ref[0] = v


def _inproj(x, g, w, bd, qn, kn, *, tm, p0):
    nb, rows, _ = x.shape
    grid = (nb, rows // tm)

    def tile(width):
        return pl.BlockSpec((1, tm, width), lambda b, i: (b, i, 0))

    def const(shape):
        return pl.BlockSpec(shape, lambda b, i: (0,) * len(shape), pipeline_mode=pl.Buffered(1))

    widths = (384, 896, DT_COLS, 384, 384, 384, 384, 384, 256, 256)
    dtypes = (F32, F32, F32, BF16, BF16, BF16, F32, F32, F32, F32)
    return pl.pallas_call(
        functools.partial(_inproj_kernel, tm=tm, p0=p0),
        grid=grid,
        in_specs=[tile(D_MODEL), const((1, D_MODEL)), const((D_MODEL, IN_COLS_PAD)),
                  const((384, 384)), const((1, 384)), const((1, 384))],
        out_specs=[tile(wd) for wd in widths],
        out_shape=[jax.ShapeDtypeStruct((nb, rows, wd), dt) for wd, dt in zip(widths, dtypes)],
        compiler_params=pltpu.CompilerParams(
            dimension_semantics=("parallel", "parallel"), vmem_limit_bytes=VMEM_LIMIT),
        name="inproj",
    )(x, g, w, bd, qn, kn)


def _ssd_kernel(xbc_ref, z_ref, dt_ref, cw_ref, cb_ref, dtb_ref, alog_ref, dvec_ref, nw_ref,
                y_ref, st_ref, tail_ref, xbuf, hst, *, c, p0):
    ci = pl.program_id(1)

    @pl.when(ci == 0)
    def _():
        xbuf[0:SUBLANES, :] = jnp.zeros((SUBLANES, SSD_XBC), F32)
        hst[...] = jnp.zeros_like(hst)

    xbuf[SUBLANES:SUBLANES + c, :] = xbc_ref[0]
    conv = cb_ref[...]
    for j in range(4):
        conv = conv + xbuf[5 + j:5 + j + c, :] * cw_ref[j:j + 1, :]
    last = xbuf[c:c + SUBLANES, :]
    tail_ref[0] = last
    xbuf[0:SUBLANES, :] = last
    act = _silu(conv)
    xs = act[:, 0:384]
    bm = act[:, 384:640]
    cm = act[:, 640:896]

    rows = ci * c + _iota((c, 1), 0)
    dt = _softplus(dt_ref[0] + dtb_ref[...])
    dt = jnp.where(rows >= p0, dt, 0.0)
    a = dt * (-jnp.exp(alog_ref[...]))
    r_i = _iota((c, c), 0)
    c_i = _iota((c, c), 1)
    tri = r_i >= c_i
    cum = _dot_split_rhs(tri.astype(BF16), a, 3)
    cum_t = cum.T
    cum_last = cum[c - 1:c, :]
    e_cum = jnp.exp(cum)
    dec = jnp.exp(cum_last - cum) * dt
    e_last = jnp.exp(cum_last)

    lane = _iota((1, LANES), 1)
    lo_half = lane < HEAD_DIM
    rowp = _iota((LANES, 1), 0)
    top_half = rowp < HEAD_DIM
    cbs = [_dot_nt(cm[:, g * LANES:(g + 1) * LANES], bm[:, g * LANES:(g + 1) * LANES]) for g in range(2)]

    def col(m, h):
        return m[:, h:h + 1]

    ys = []
    for pr in range(3):
        ha, hb = 2 * pr, 2 * pr + 1
        ga, gb = ha // 3, hb // 3
        xp = xs[:, pr * LANES:(pr + 1) * LANES]
        xd = xp * jnp.where(lo_half, col(dt, ha), col(dt, hb))
        hpair = hst[pr * LANES:(pr + 1) * LANES, :]
        y_intra = []
        for h, g in ((ha, ga), (hb, gb)):
            seg = col(cum, h) - cum_t[h:h + 1, :]
            lm = jnp.exp(jnp.where(tri, seg, -jnp.inf))
            y_intra.append(_dot(cbs[g] * lm, xd))
        yi_a = _dot_nt(cm[:, ga * LANES:(ga + 1) * LANES], hpair)
        yi_b = yi_a if gb == ga else _dot_nt(cm[:, gb * LANES:(gb + 1) * LANES], hpair)
        y = jnp.where(lo_half, y_intra[0] + yi_a * col(e_cum, ha), y_intra[1] + yi_b * col(e_cum, hb))
        ys.append(y + xp * dvec_ref[:, pr * LANES:(pr + 1) * LANES])
        xe_t = (xp * jnp.where(lo_half, col(dec, ha), col(dec, hb))).T
        hn_a = _dot(xe_t, bm[:, ga * LANES:(ga + 1) * LANES])
        hn_b = hn_a if gb == ga else _dot(xe_t, bm[:, gb * LANES:(gb + 1) * LANES])
        keep = jnp.where(top_half, col(e_last, ha), col(e_last, hb))
        hst[pr * LANES:(pr + 1) * LANES, :] = hpair * keep + jnp.where(top_half, hn_a, hn_b)

    y = jnp.concatenate(ys, axis=-1) * _silu(z_ref[0])
    y_ref[0] = _rms(y, nw_ref[...]).astype(BF16)

    @pl.when(ci == pl.num_programs(1) - 1)
    def _():
        st_ref[0] = hst[...]


def _ssd(xbc, z, dt, cw, cb, dtb, alog, dvec, nw, *, c, p0):
    nb, rows, _ = xbc.shape

    def tile(width):
        return pl.BlockSpec((1, c, width), lambda b, i: (b, i, 0))

    def const(shape):
        return pl.BlockSpec(shape, lambda b, i: (0,) * len(shape))

    return pl.pallas_call(
        functools.partial(_ssd_kernel, c=c, p0=p0),
        grid=(nb, rows // c),
        in_specs=[tile(SSD_XBC), tile(SSD_WIDTH), tile(DT_COLS), const((4, SSD_XBC)), const((1, SSD_XBC)),
                  const((1, DT_COLS)), const((1, DT_COLS)), const((1, SSD_WIDTH)), const((1, SSD_WIDTH))],
        out_specs=[tile(SSD_WIDTH),
                   pl.BlockSpec((1, SSD_WIDTH, SSD_STATE), lambda b, i: (b, 0, 0)),
                   pl.BlockSpec((1, SUBLANES, SSD_XBC), lambda b, i: (b, 0, 0))],
        out_shape=[jax.ShapeDtypeStruct((nb, rows, SSD_WIDTH), BF16),
                   jax.ShapeDtypeStruct((nb, SSD_WIDTH, SSD_STATE), F32),
                   jax.ShapeDtypeStruct((nb, SUBLANES, SSD_XBC), F32)],
        scratch_shapes=[pltpu.VMEM((SUBLANES + c, SSD_XBC), F32), pltpu.VMEM((SSD_WIDTH, SSD_STATE), F32)],
        compiler_params=pltpu.CompilerParams(
            dimension_semantics=("arbitrary", "arbitrary"), vmem_limit_bytes=VMEM_LIMIT),
        name="ssd_scan",
    )(xbc, z, dt, cw, cb, dtb, alog, dvec, nw)


def _rg_gates(xc, wax_ref, bax_ref, lam_ref):
    ri = jax.nn.sigmoid(_dot(xc, wax_ref[...]) + bax_ref[...])
    r = ri[:, 0:RG_WIDTH]
    i = ri[:, RG_WIDTH:2 * RG_WIDTH]
    log_a = RG_C * r * (-_softplus(-lam_ref[...]))
    a = jnp.exp(log_a)
    u = jnp.sqrt(1.0 - jnp.exp(2.0 * log_a)) * (i * xc)
    return a, u


def _rg_kernel(xr_ref, gr_ref, cw_ref, cb_ref, wax_ref, bax_ref, lam_ref, nw_ref,
               y_ref, h_ref, tail_ref, xbuf, hprev, *, c, p0):
    ci = pl.program_id(1)

    @pl.when(ci == 0)
    def _():
        xbuf[0:SUBLANES, :] = jnp.zeros((SUBLANES, RG_WIDTH), F32)
        hprev[...] = jnp.zeros_like(hprev)

    xbuf[SUBLANES:SUBLANES + c, :] = xr_ref[0]
    xc = cb_ref[...]
    for j in range(4):
        xc = xc + xbuf[5 + j:5 + j + c, :] * cw_ref[j:j + 1, :]
    last = xbuf[c:c + SUBLANES, :]
    tail_ref[0] = last
    xbuf[0:SUBLANES, :] = last

    a, u = _rg_gates(xc, wax_ref, bax_ref, lam_ref)
    rowi = _iota((c, 1), 0)
    valid = (ci * c + rowi) >= p0
    a = jnp.where(valid, a, 1.0)
    u = jnp.where(valid, u, 0.0)
    d = 1
    while d < c:
        a_sh = jnp.where(rowi >= d, pltpu.roll(a, d, 0), 1.0)
        u_sh = jnp.where(rowi >= d, pltpu.roll(u, d, 0), 0.0)
        u = a * u_sh + u
        a = a * a_sh
        d *= 2
    h = a * hprev[...] + u
    h_last = h[c - 1:c, :]
    hprev[...] = h_last
    h_ref[0] = h_last
    y = h * _gelu_tanh(gr_ref[0])
    y_ref[0] = _rms(y, nw_ref[...]).astype(BF16)


def _rg(xr, gr, cw, cb, wax, bax, lam, nw, *, c, p0):
    nb, rows, _ = xr.shape

    def tile():
        return pl.BlockSpec((1, c, RG_WIDTH), lambda b, i: (b, i, 0))

    def const(shape):
        return pl.BlockSpec(shape, lambda b, i: (0,) * len(shape))

    return pl.pallas_call(
        functools.partial(_rg_kernel, c=c, p0=p0),
        grid=(nb, rows // c),
        in_specs=[tile(), tile(), const((4, RG_WIDTH)), const((1, RG_WIDTH)), const((RG_WIDTH, 2 * RG_WIDTH)),
                  const((1, 2 * RG_WIDTH)), const((1, RG_WIDTH)), const((1, RG_WIDTH))],
        out_specs=[tile(),
                   pl.BlockSpec((1, 1, RG_WIDTH), lambda b, i: (b, 0, 0)),
                   pl.BlockSpec((1, SUBLANES, RG_WIDTH), lambda b, i: (b, 0, 0))],
        out_shape=[jax.ShapeDtypeStruct((nb, rows, RG_WIDTH), BF16),
                   jax.ShapeDtypeStruct((nb, 1, RG_WIDTH), F32),
                   jax.ShapeDtypeStruct((nb, SUBLANES, RG_WIDTH), F32)],
        scratch_shapes=[pltpu.VMEM((SUBLANES + c, RG_WIDTH), F32), pltpu.VMEM((1, RG_WIDTH), F32)],
        compiler_params=pltpu.CompilerParams(dimension_semantics=("arbitrary", "arbitrary")),
        name="rglru_scan",
    )(xr, gr, cw, cb, wax, bax, lam, nw)


def _log2_one_minus_beta(zn):
    return jnp.minimum(zn, 0.0) - jnp.log2(1.0 + jnp.exp2(-jnp.abs(zn)))


def _attn_kernel(q_ref, k_ref, v_ref, kb_ref, o_ref, zraw, zbuf, lbuf, sbuf, tbuf, wbuf, cbuf, acc, qsb, umat,
                 *, t):
    qi = pl.program_id(2)
    n = jnp.maximum(qi + 1, 4)
    q = q_ref[0]
    lane = _iota((1, PAIR), 1)
    lo_half = lane < HEAD_DIM
    zero = jnp.zeros((), BF16)
    r_i = _iota((t, t), 0)
    c_i = _iota((t, t), 1)
    qsb[0] = jnp.where(lo_half, q, zero)
    qsb[1] = jnp.where(lo_half, zero, q)
    umat[...] = (r_i >= c_i).astype(BF16)

    cbuf[...] = jnp.zeros_like(cbuf)
    acc[...] = jnp.zeros_like(acc)

    def stage_qk(b):
        j = jnp.maximum(qi - b, 0)
        ks = k_ref[0, pl.ds(pl.multiple_of(j * t, t), t), :]
        for hh in range(2):
            zraw[hh] = _dot_nt(qsb[hh], ks)

    def stage_a(b, diag=False, maybe_masked=False):
        j = qi - b
        if maybe_masked:
            dummy = j < 0
            j = jnp.maximum(j, 0)
        kbias = kb_ref[0, j]
        if maybe_masked:
            kbias = kbias + jnp.where(dummy, -NEG_BIG, 0.0)
        for hh in range(2):
            zn = zraw[hh] + kbias[hh:hh + 1, :]
            if diag:
                zn = zn + jnp.where(c_i < r_i, 0.0, -NEG_BIG)
            zbuf[hh] = zn
            lbuf[hh] = _log2_one_minus_beta(zn).astype(BF16)

    def stage_b():
        for hh in range(2):
            incl = _dot(lbuf[hh], umat[...])
            sbuf[hh] = incl - zbuf[hh]
            tbuf[hh] = incl[:, 0:1]

    def stage_c1():
        for hh in range(2):
            c = cbuf[hh]
            wbuf[hh] = jnp.exp2(sbuf[hh] + c).astype(BF16)
            cbuf[hh] = c + tbuf[hh]

    def stage_c2(b):
        j = jnp.maximum(qi - b, 0)
        vs = v_ref[0, pl.ds(pl.multiple_of(j * t, t), t), :]
        acc[...] += (_dot(wbuf[0], jnp.where(lo_half, vs, zero))
                     + _dot(wbuf[1], jnp.where(lo_half, zero, vs)))

    stage_qk(0)
    stage_a(0, diag=True)
    stage_qk(1)
    stage_b()
    stage_a(1, maybe_masked=True)
    stage_qk(2)
    stage_c1()
    stage_b()
    stage_a(2, maybe_masked=True)
    stage_qk(3)

    def body(i, carry):
        stage_c2(i - 4)
        stage_c1()
        stage_b()
        stage_a(i - 1)
        stage_qk(i)
        return carry

    lax.fori_loop(4, n, body, 0)
    stage_c2(n - 4)
    stage_c1()
    stage_b()
    stage_a(n - 1, maybe_masked=True)
    stage_c2(n - 3)
    stage_c1()
    stage_b()
    stage_c2(n - 2)
    stage_c1()
    stage_c2(n - 1)
    o_ref[0] = acc[...]


def _attn(qb, kb, vb, kbias, *, t):
    nb, rows, _ = qb.shape
    nk = rows // t
    return pl.pallas_call(
        functools.partial(_attn_kernel, t=t),
        grid=(nb, SB_WIDTH // PAIR, nk),
        in_specs=[pl.BlockSpec((1, t, PAIR), lambda b, p, i: (b, i, p)),
                  pl.BlockSpec((1, rows, PAIR), lambda b, p, i: (b, 0, p)),
                  pl.BlockSpec((1, rows, PAIR), lambda b, p, i: (b, 0, p)),
                  pl.BlockSpec((1, nk, SUBLANES, t), lambda b, p, i: (p, 0, 0, 0))],
        out_specs=pl.BlockSpec((1, t, PAIR), lambda b, p, i: (b, i, p)),
        out_shape=jax.ShapeDtypeStruct((nb, rows, SB_WIDTH), F32),
        scratch_shapes=[pltpu.VMEM((2, t, t), F32), pltpu.VMEM((2, t, t), F32), pltpu.VMEM((2, t, t), BF16),
                        pltpu.VMEM((2, t, t), F32), pltpu.VMEM((2, t, 1), F32), pltpu.VMEM((2, t, t), BF16),
                        pltpu.VMEM((2, t, 1), F32), pltpu.VMEM((t, PAIR), F32),
                        pltpu.VMEM((2, t, PAIR), BF16), pltpu.VMEM((t, t), BF16)],
        compiler_params=pltpu.CompilerParams(
            dimension_semantics=("arbitrary", "arbitrary", "arbitrary"), vmem_limit_bytes=VMEM_LIMIT),
        name="sb_attention",
    )(qb, kb, vb, kbias)


def _ffn_kernel(x_ref, yssd_ref, ysb_ref, yrg_ref, sbn_ref, wo_ref, n2_ref, wup_ref, cw_ref, cb_ref, wd_ref,
                out_ref, tail_ref, gbuf, *, tm, p0):
    i = pl.program_id(1)

    @pl.when(i == 0)
    def _():
        gbuf[0:SUBLANES, :] = jnp.zeros((SUBLANES, D_FF), F32)

    ysb = _rms(ysb_ref[0], sbn_ref[...]).astype(BF16)
    mix = jnp.concatenate([yssd_ref[0], ysb, yrg_ref[0]], axis=-1)
    xm = x_ref[0] + _dot(mix, wo_ref[...])
    h2 = _rms(xm, n2_ref[...])
    if p0 > 0:
        rows = i * tm + _iota((tm, 1), 0)
        h2 = jnp.where(rows >= p0, h2, 0.0)
    gu = _dot(h2.astype(BF16), wup_ref[...])
    gbuf[SUBLANES:SUBLANES + tm, :] = gu[:, 0:D_FF]
    conv = cb_ref[...]
    for j in range(3):
        conv = conv + gbuf[6 + j:6 + j + tm, :] * cw_ref[j:j + 1, :]
    last = gbuf[tm:tm + SUBLANES, :]
    tail_ref[0] = last
    gbuf[0:SUBLANES, :] = last
    act = (_silu(conv) * gu[:, D_FF:2 * D_FF]).astype(BF16)
    out_ref[0] = xm + _dot(act, wd_ref[...])


def _ffn(x, yssd, ysb, yrg, sbn, wo, n2, wup, cw, cb, wd, *, tm, p0):
    nb, rows, _ = x.shape

    def tile(width):
        return pl.BlockSpec((1, tm, width), lambda b, i: (b, i, 0))

    def const(shape):
        return pl.BlockSpec(shape, lambda b, i: (0,) * len(shape), pipeline_mode=pl.Buffered(1))

    return pl.pallas_call(
        functools.partial(_ffn_kernel, tm=tm, p0=p0),
        grid=(nb, rows // tm),
        in_specs=[tile(D_MODEL), tile(SSD_WIDTH), tile(SB_WIDTH), tile(RG_WIDTH), const((1, SB_WIDTH)),
                  const((D_MODEL, D_MODEL)), const((1, D_MODEL)), const((D_MODEL, 2 * D_FF)),
                  const((3, D_FF)), const((1, D_FF)), const((D_FF, D_MODEL))],
        out_specs=[tile(D_MODEL), pl.BlockSpec((1, SUBLANES, D_FF), lambda b, i: (b, 0, 0))],
        out_shape=[jax.ShapeDtypeStruct((nb, rows, D_MODEL), F32),
                   jax.ShapeDtypeStruct((nb, SUBLANES, D_FF), F32)],
        scratch_shapes=[pltpu.VMEM((SUBLANES + tm, D_FF), F32)],
        compiler_params=pltpu.CompilerParams(
            dimension_semantics=("arbitrary", "arbitrary"), vmem_limit_bytes=VMEM_LIMIT),
        name="outproj_ffn",
    )(x, yssd, ysb, yrg, sbn, wo, n2, wup, cw, cb, wd)


def _sample_mix_kernel(z_ref, xbc_ref, dt_ref, sbuf_ref, st_ref, xr_ref, gr_ref, rbuf_ref, rh_ref,
                       scw_ref, scb_ref, dtb_ref, alog_ref, dvec_ref, snw_ref,
                       rcw_ref, rcb_ref, wax_ref, bax_ref, lam_ref, rnw_ref,
                       yssd_ref, st_out_ref, sbuf_out_ref, yrg_ref, rh_out_ref, rbuf_out_ref,
                       yscr, *, nseq):
    xbc = xbc_ref[...]
    conv = scb_ref[...] + xbc * scw_ref[3:4, :]
    for j in range(3):
        conv = conv + sbuf_ref[j] * scw_ref[j:j + 1, :]
    sbuf_out_ref[0] = sbuf_ref[1]
    sbuf_out_ref[1] = sbuf_ref[2]
    sbuf_out_ref[2] = xbc
    act = _silu(conv)
    xs = act[:, 0:384]
    bm = act[:, 384:640]
    cm = act[:, 640:896]
    dt = _softplus(dt_ref[...] + dtb_ref[...])
    da = jnp.exp(dt * (-jnp.exp(alog_ref[...])))

    lane = _iota((1, LANES), 1)
    lo_half = lane < HEAD_DIM
    rowp = _iota((LANES, 1), 0)
    top_half = rowp < HEAD_DIM
    pad_rows = LANES - nseq
    yscr[...] = jnp.zeros_like(yscr)
    for pr in range(3):
        ha, hb = 2 * pr, 2 * pr + 1
        ga, gb = ha // 3, hb // 3
        xp = xs[:, pr * LANES:(pr + 1) * LANES]
        xd = xp * jnp.where(lo_half, dt[:, ha:ha + 1], dt[:, hb:hb + 1])
        xd_t = jnp.concatenate([xd, jnp.zeros((pad_rows, LANES), F32)], axis=0).T
        for b in range(nseq):
            hpair = st_ref[b, pr * LANES:(pr + 1) * LANES, :]
            keep = jnp.where(top_half, da[b:b + 1, ha:ha + 1], da[b:b + 1, hb:hb + 1])
            brow = jnp.where(top_half, bm[b:b + 1, ga * LANES:(ga + 1) * LANES],
                             bm[b:b + 1, gb * LANES:(gb + 1) * LANES])
            hnew = hpair * keep + xd_t[:, b:b + 1] * brow
            st_out_ref[b, pr * LANES:(pr + 1) * LANES, :] = hnew
            crow = jnp.where(top_half, cm[b:b + 1, ga * LANES:(ga + 1) * LANES],
                             cm[b:b + 1, gb * LANES:(gb + 1) * LANES])
            ycol = jnp.sum(hnew * crow, axis=1, keepdims=True)
            yscr[pr, :, b:b + 1] = ycol
    ys = []
    for pr in range(3):
        ys.append(yscr[pr].T[0:nseq, :])
    y = jnp.concatenate(ys, axis=-1) + xs * dvec_ref[...]
    y = y * _silu(z_ref[...])
    yssd_ref[...] = _rms(y, snw_ref[...]).astype(BF16)

    xr = xr_ref[...]
    xc = rcb_ref[...] + xr * rcw_ref[3:4, :]
    for j in range(3):
        xc = xc + rbuf_ref[j] * rcw_ref[j:j + 1, :]
    rbuf_out_ref[0] = rbuf_ref[1]
    rbuf_out_ref[1] = rbuf_ref[2]
    rbuf_out_ref[2] = xr
    a, u = _rg_gates(xc, wax_ref, bax_ref, lam_ref)
    h = a * rh_ref[...] + u
    rh_out_ref[...] = h
    yrg_ref[...] = _rms(h * _gelu_tanh(gr_ref[...]), rnw_ref[...]).astype(BF16)


def _sample_mix(z, xbc, dt, sbuf, st, xr, gr, rbuf, rh, scw, scb, dtb, alog, dvec, snw,
                rcw, rcb, wax, bax, lam, rnw):
    nseq = z.shape[0]
    return pl.pallas_call(
        functools.partial(_sample_mix_kernel, nseq=nseq),
        out_shape=[jax.ShapeDtypeStruct((nseq, SSD_WIDTH), BF16),
                   jax.ShapeDtypeStruct((nseq, SSD_WIDTH, SSD_STATE), F32),
                   jax.ShapeDtypeStruct((3, nseq, SSD_XBC), F32),
                   jax.ShapeDtypeStruct((nseq, RG_WIDTH), BF16),
                   jax.ShapeDtypeStruct((nseq, RG_WIDTH), F32),
                   jax.ShapeDtypeStruct((3, nseq, RG_WIDTH), F32)],
        scratch_shapes=[pltpu.VMEM((3, LANES, LANES), F32)],
        compiler_params=pltpu.CompilerParams(vmem_limit_bytes=VMEM_LIMIT),
        name="sample_mixers",
    )(z, xbc, dt, sbuf, st, xr, gr, rbuf, rh, scw, scb, dtb, alog, dvec, snw, rcw, rcb, wax, bax, lam, rnw)


def _sample_attn_kernel(pt_ref, q_ref, bias_ref, *refs, npg):
    del pt_ref
    k_refs = refs[0:npg]
    v_refs = refs[npg:2 * npg]
    o_ref = refs[2 * npg]
    carry_ref, acc_ref = refs[2 * npg + 1:]
    g = pl.program_id(1)

    @pl.when(g == 0)
    def _():
        carry_ref[...] = jnp.zeros_like(carry_ref)
        acc_ref[...] = jnp.zeros_like(acc_ref)

    qb = q_ref[0]
    r_i = _iota((PAGE, PAGE), 0)
    c_i = _iota((PAGE, PAGE), 1)
    incl_mat = (r_i >= c_i).astype(BF16)
    pad = jnp.zeros((SUBLANES - SB_HEADS, PAGE), F32)
    zs, ls = [], []
    for r in range(npg):
        prod = k_refs[r][0] * qb
        rows = [jnp.sum(prod[h * HEAD_DIM:(h + 1) * HEAD_DIM, :], axis=0, keepdims=True)
                for h in range(SB_HEADS)]
        zn = jnp.concatenate(rows + [pad], axis=0) + bias_ref[...]
        zs.append(zn)
        ls.append(_log2_one_minus_beta(zn).astype(BF16))
    carry = carry_ref[...]
    for r in range(npg):
        incl = _dot(ls[r], incl_mat)
        w = jnp.exp2(incl - zs[r] + carry)
        for h in range(SB_HEADS):
            hs = slice(h * HEAD_DIM, (h + 1) * HEAD_DIM)
            acc_ref[hs, :] += jnp.broadcast_to(w[h:h + 1, :], (HEAD_DIM, PAGE)) * v_refs[r][0, hs, :]
        carry = carry + incl[:, 0:1]
    carry_ref[...] = carry

    @pl.when(g == pl.num_programs(1) - 1)
    def _():
        o_ref[0] = jnp.sum(acc_ref[...], axis=1, keepdims=True)


def _sample_attn(pt_flat, q3, bias8, ck, cv, *, nseq, n_pages, npg):
    steps = n_pages // npg

    def page_spec(r):
        def imap(b, g, pt):
            return (pt[b * n_pages + (n_pages - 1 - (g * npg + r))], 0, 0)
        return pl.BlockSpec((1, SB_WIDTH, PAGE), imap)

    grid_spec = pltpu.PrefetchScalarGridSpec(
        num_scalar_prefetch=1,
        grid=(nseq, steps),
        in_specs=[pl.BlockSpec((1, SB_WIDTH, PAGE), lambda b, g, pt: (b, 0, 0)),
                  pl.BlockSpec((SUBLANES, PAGE), lambda b, g, pt: (0, 0))]
                 + [page_spec(r) for r in range(npg)] + [page_spec(r) for r in range(npg)],
        out_specs=pl.BlockSpec((1, SB_WIDTH, 1), lambda b, g, pt: (b, 0, 0)),
        scratch_shapes=[pltpu.VMEM((SUBLANES, 1), F32), pltpu.VMEM((SB_WIDTH, PAGE), F32)],
    )
    return pl.pallas_call(
        functools.partial(_sample_attn_kernel, npg=npg),
        grid_spec=grid_spec,
        out_shape=jax.ShapeDtypeStruct((nseq, SB_WIDTH, 1), F32),
        compiler_params=pltpu.CompilerParams(dimension_semantics=("arbitrary", "arbitrary")),
        name="sample_attention",
    )(pt_flat, q3, bias8, *([ck] * npg), *([cv] * npg))


def _sample_ffn_kernel(x_ref, yssd_ref, ysb_ref, yrg_ref, fbuf_ref, sbn_ref, wo_ref, n2_ref, wup_ref,
                       cw_ref, cb_ref, wd_ref, out_ref, fbuf_out_ref):
    ysb = _rms(ysb_ref[...], sbn_ref[...]).astype(BF16)
    mix = jnp.concatenate([yssd_ref[...], ysb, yrg_ref[...]], axis=-1)
    xm = x_ref[...] + _dot(mix, wo_ref[...])
    h2 = _rms(xm, n2_ref[...])
    gu = _dot(h2.astype(BF16), wup_ref[...])
    g = gu[:, 0:D_FF]
    conv = cb_ref[...] + fbuf_ref[0] * cw_ref[0:1, :] + fbuf_ref[1] * cw_ref[1:2, :] + g * cw_ref[2:3, :]
    fbuf_out_ref[0] = fbuf_ref[1]
    fbuf_out_ref[1] = g
    act = (_silu(conv) * gu[:, D_FF:2 * D_FF]).astype(BF16)
    out_ref[...] = xm + _dot(act, wd_ref[...])


def _sample_ffn(x, yssd, ysb, yrg, fbuf, sbn, wo, n2, wup, cw, cb, wd):
    nseq = x.shape[0]
    return pl.pallas_call(
        _sample_ffn_kernel,
        out_shape=[jax.ShapeDtypeStruct((nseq, D_MODEL), F32),
                   jax.ShapeDtypeStruct((2, nseq, D_FF), F32)],
        compiler_params=pltpu.CompilerParams(vmem_limit_bytes=VMEM_LIMIT),
        name="sample_outproj_ffn",
    )(x, yssd, ysb, yrg, fbuf, sbn, wo, n2, wup, cw, cb, wd)


def _pad_cols(a, width):
    return jnp.pad(a, ((0, 0), (0, width - a.shape[-1])))


def _block_diag(w):
    nblk, bi, bj = w.shape
    out = jnp.zeros((nblk * bi, nblk * bj), w.dtype)
    for h in range(nblk):
        out = out.at[h * bi:(h + 1) * bi, h * bj:(h + 1) * bj].set(w[h])
    return out


def kernel(x_prompt, x_sample, cache_k, cache_v, page_table, state_ssm, state_ssm_conv, state_rg, state_rg_conv, state_ffn_conv, meta_tokens, norm1, w_in, ssd_conv_w, ssd_conv_b, ssd_dt_bias, ssd_a_log, ssd_d, ssd_norm, q_norm, k_norm, sb_bias, sb_out_norm, rg_conv_w, rg_conv_b, rg_wa, rg_ba, rg_wx, rg_bx, rg_lambda, rg_out_norm, w_out, norm2, w_up, ffn_conv_w, ffn_conv_b, w_down):
    depth = w_in.shape[0]
    nb, seq, _ = x_prompt.shape
    nseq = x_sample.shape[0]
    n_pool = cache_k.shape[1]
    n_pages = page_table.shape[1]
    t_real = N_META + seq
    t_pad = -(-t_real // ATT_TILE) * ATT_TILE
    p0 = t_pad - t_real
    nk = t_pad // ATT_TILE

    meta = jnp.broadcast_to(meta_tokens[None], (nb, N_META, D_MODEL))
    xp = jnp.concatenate([jnp.zeros((nb, p0, D_MODEL), F32), meta, x_prompt], axis=1)
    xs = x_sample.reshape(1, nseq, D_MODEL)

    head_of = np.arange(SB_WIDTH) // HEAD_DIM
    bd = jnp.asarray(head_of[:, None] == head_of[None, :], BF16)
    key_valid = jnp.arange(t_pad) >= p0
    ck = cache_k.transpose(0, 1, 3, 4, 2).reshape(depth * n_pool, SB_WIDTH, PAGE)
    cv = cache_v.transpose(0, 1, 3, 4, 2).reshape(depth * n_pool, SB_WIDTH, PAGE)

    outs_p, outs_s = [], []
    for l in range(depth):
        wz, wxbc, wdt, wq, wk, wv, wxr, wgr = jnp.split(
            w_in[l], np.cumsum([384, 896, 6, 384, 384, 384, 256])[:].tolist(), axis=1)
        w1 = jnp.concatenate([wz, wxbc, wq, wk, wv, wxr, wgr, _pad_cols(wdt, DT_COLS)], axis=1).astype(BF16)
        g1 = norm1[l][None]
        qn = jnp.tile(q_norm[l], SB_HEADS)[None]
        kn = jnp.tile(k_norm[l], SB_HEADS)[None]
        dtb = _pad_cols(ssd_dt_bias[l][None], DT_COLS)
        alog = _pad_cols(ssd_a_log[l][None], DT_COLS)
        dvec = jnp.repeat(ssd_d[l], HEAD_DIM)[None]
        snw = ssd_norm[l][None]
        scw, scb = ssd_conv_w[l], ssd_conv_b[l][None]
        rcw, rcb = rg_conv_w[l], rg_conv_b[l][None]
        wax = jnp.concatenate([_block_diag(rg_wa[l]), _block_diag(rg_wx[l])], axis=1)
        bax = jnp.concatenate([rg_ba[l].reshape(1, RG_WIDTH), rg_bx[l].reshape(1, RG_WIDTH)], axis=1)
        lam = rg_lambda[l][None]
        rnw = rg_out_norm[l][None]
        sbn = sb_out_norm[l][None]
        wo = w_out[l].astype(BF16)
        n2 = norm2[l][None]
        wup = w_up[l].astype(BF16)
        fcw, fcb = ffn_conv_w[l], ffn_conv_b[l][None]
        wd = w_down[l].astype(BF16)

        z, xbc, dt, qb, kb, vb, kf, vf, xr, gr = _inproj(xp, g1, w1, bd, qn, kn, tm=ROW_TILE, p0=p0)
        yssd, ssm_p, sconv_p = _ssd(xbc, z, dt, scw, scb, dtb, alog, dvec, snw, c=CHUNK, p0=p0)
        yrg, rgh_p, rconv_p = _rg(xr, gr, rcw, rcb, wax, bax, lam, rnw, c=CHUNK, p0=p0)
        neg_bias = -LOG2E * sb_bias[l]
        kbias = jnp.where(key_valid[None, :], neg_bias[:, None], -NEG_BIG)
        kbias = kbias.reshape(3, 2, nk, ATT_TILE).transpose(0, 2, 1, 3)
        kbias = jnp.pad(kbias, ((0, 0), (0, 0), (0, SUBLANES - 2), (0, 0)))
        ysb = _attn(qb, kb, vb, kbias, t=ATT_TILE)
        xp, fconv_p = _ffn(xp, yssd, ysb, yrg, sbn, wo, n2, wup, fcw, fcb, wd, tm=ROW_TILE, p0=p0)
        outs_p.append((kf[:, p0:].reshape(nb, t_real, SB_HEADS, HEAD_DIM),
                       vf[:, p0:].reshape(nb, t_real, SB_HEADS, HEAD_DIM),
                       ssm_p.reshape(nb, SSD_HEADS, HEAD_DIM, SSD_STATE),
                       sconv_p[:, SUBLANES - 3:],
                       rgh_p.reshape(nb, RG_WIDTH),
                       rconv_p[:, SUBLANES - 3:],
                       fconv_p[:, SUBLANES - 2:]))

        z, xbc, dt, qb, kb, vb, kf, vf, xr, gr = _inproj(xs, g1, w1, bd, qn, kn, tm=nseq, p0=0)
        yssd, ssm_s, sconv_s, yrg, rgh_s, rconv_s = _sample_mix(
            z[0], xbc[0], dt[0], state_ssm_conv[l].transpose(1, 0, 2),
            state_ssm[l].reshape(nseq, SSD_WIDTH, SSD_STATE), xr[0], gr[0],
            state_rg_conv[l].transpose(1, 0, 2), state_rg[l],
            scw, scb, dtb, alog, dvec, snw, rcw, rcb, wax, bax, lam, rnw)
        bias8 = jnp.broadcast_to(jnp.pad(neg_bias, (0, SUBLANES - SB_HEADS))[:, None], (SUBLANES, PAGE))
        pt_flat = (page_table + l * n_pool).reshape(-1).astype(jnp.int32)
        q_lanes = jnp.broadcast_to(qb[0].astype(F32)[:, :, None], (nseq, SB_WIDTH, PAGE))
        ysb = _sample_attn(pt_flat, q_lanes, bias8, ck, cv,
                           nseq=nseq, n_pages=n_pages, npg=PAGES_PER_STEP)
        xs2, fconv_s = _sample_ffn(xs[0], yssd, ysb.reshape(nseq, SB_WIDTH), yrg,
                                   state_ffn_conv[l].transpose(1, 0, 2), sbn, wo, n2, wup, fcw, fcb, wd)
        xs = xs2[None]
        outs_s.append((kf.reshape(nseq, 1, SB_HEADS, HEAD_DIM),
                       vf.reshape(nseq, 1, SB_HEADS, HEAD_DIM),
                       ssm_s.reshape(nseq, SSD_HEADS, HEAD_DIM, SSD_STATE),
                       sconv_s.transpose(1, 0, 2),
                       rgh_s,
                       rconv_s.transpose(1, 0, 2),
                       fconv_s.transpose(1, 0, 2)))

    def stk(outs, i):
        return jnp.stack([o[i] for o in outs], axis=0)

    y_prompt = xp[:, p0 + N_META:]
    y_sample = xs.reshape(nseq, 1, D_MODEL)
    return (y_prompt, y_sample,
            stk(outs_p, 0), stk(outs_p, 1), stk(outs_p, 2), stk(outs_p, 3), stk(outs_p, 4), stk(outs_p, 5), stk(outs_p, 6),
            stk(outs_s, 0), stk(outs_s, 1), stk(outs_s, 2), stk(outs_s, 3), stk(outs_s, 4), stk(outs_s, 5), stk(outs_s, 6))
```

```python
import functools

import jax
import jax.numpy as jnp
import numpy as np
from jax import lax
from jax.experimental import pallas as pl
from jax.experimental.pallas import tpu as pltpu

F32 = jnp.float32
BF16 = jnp.bfloat16

D_MODEL = 1024
N_META = 16
HEAD_DIM = 64
SSD_WIDTH = 384
SSD_HEADS = 6
SSD_STATE = 128
SSD_XBC = 896
SB_WIDTH = 384
SB_HEADS = 6
RG_WIDTH = 256
RG_C = 8.0
D_FF = 2816
EPS = 1e-6

LANES = 128
SUBLANES = 8
PAIR = 2 * HEAD_DIM
DT_COLS = LANES
IN_COLS_PAD = 2 * SSD_WIDTH + 512 + 3 * SB_WIDTH + 2 * RG_WIDTH + DT_COLS
NEG_BIG = -1e30
LOG2E = 1.4426950408889634

ROW_TILE = 256
CHUNK = 128
ATT_TQ = 256
ATT_TK = 256
PAGE = 128
PAGES_PER_STEP = 16
VMEM_LIMIT = 56 * 1024 * 1024

NT_DIMS = (((1,), (1,)), ((), ()))


def _dot(a, b):
    return jnp.dot(a, b, preferred_element_type=F32)


def _dot_nt(a, b):
    return lax.dot_general(a, b, NT_DIMS, preferred_element_type=F32)


def _split_bf16(x, n):
    parts = []
    r = x
    for _ in range(n):
        p = r.astype(BF16)
        parts.append(p)
        r = r - p.astype(F32)
    return parts


def _dot_split_lhs(x, m, n):
    out = None
    for p in _split_bf16(x, n):
        t = _dot(p, m)
        out = t if out is None else out + t
    return out


def _dot_split_rhs(m, x, n):
    out = None
    for p in _split_bf16(x, n):
        t = _dot(m, p)
        out = t if out is None else out + t
    return out


def _rms(x, g):
    return x * lax.rsqrt(jnp.mean(x * x, axis=-1, keepdims=True) + EPS) * g


def _softplus(x):
    return jnp.maximum(x, 0.0) + jnp.log(1.0 + jnp.exp(-jnp.abs(x)))


def _silu(x):
    return x * jax.nn.sigmoid(x)


def _gelu_tanh(x):
    return 0.5 * x * (1.0 + jnp.tanh(0.7978845608028654 * (x + 0.044715 * (x * x * x))))


def _iota(shape, dim):
    return lax.broadcasted_iota(jnp.int32, shape, dim)


def _inproj_kernel(x_ref, g_ref, w_ref, bd_ref, qn_ref, kn_ref,
                   z_ref, xbc_ref, dt_ref, qb_ref, kb_ref, vb_ref, k_ref, v_ref, xr_ref, gr_ref,
                   *, tm, p0):
    x = x_ref[0]
    h = _rms(x, g_ref[...])
    if p0 > 0:
        rows = pl.program_id(1) * tm + _iota((tm, 1), 0)
        h = jnp.where(rows >= p0, h, 0.0)
    proj = _dot_nt(h.astype(BF16), w_ref[...])
    z_ref[0] = proj[:, 0:384]
    xbc_ref[0] = proj[:, 384:1280]
    q = proj[:, 1280:1664]
    k = proj[:, 1664:2048]
    v = proj[:, 2048:2432]
    xr_ref[0] = proj[:, 2432:2688]
    gr_ref[0] = proj[:, 2688:2944]
    dt_ref[0] = proj[:, 2944:3072]
    bd = bd_ref[...]
    q_ms = _dot_split_lhs(q * q, bd, 2) * (1.0 / HEAD_DIM)
    k_ms = _dot_split_lhs(k * k, bd, 2) * (1.0 / HEAD_DIM)
    qn = q * lax.rsqrt(q_ms + EPS) * qn_ref[...]
    kn = k * lax.rsqrt(k_ms + EPS) * kn_ref[...]
    qb_ref[0] = (qn * (-LOG2E * HEAD_DIM ** -0.5)).astype(BF16)
    kb_ref[0] = kn.astype(BF16)
    vb_ref[0] = v.astype(BF16)
    k_ref[0] = kn
    v_ref[0] = v


def _inproj(x, g, w, bd, qn, kn, *, tm, p0):
    nb, rows, _ = x.shape
    grid = (nb, rows // tm)

    def tile(width):
        return pl.BlockSpec((1, tm, width), lambda b, i: (b, i, 0))

    def const(shape):
        return pl.BlockSpec(shape, lambda b, i: (0,) * len(shape), pipeline_mode=pl.Buffered(1))

    widths = (384, 896, DT_COLS, 384, 384, 384, 384, 384, 256, 256)
    dtypes = (F32, F32, F32, BF16, BF16, BF16, F32, F32, F32, F32)
    return pl.pallas_call(
        functools.partial(_inproj_kernel, tm=tm, p0=p0),
        grid=grid,
        in_specs=[tile(D_MODEL), const((1, D_MODEL)), const((IN_COLS_PAD, D_MODEL)),
                  const((384, 384)), const((1, 384)), const((1, 384))],
        out_specs=[tile(wd) for wd in widths],
        out_shape=[jax.ShapeDtypeStruct((nb, rows, wd), dt) for wd, dt in zip(widths, dtypes)],
        compiler_params=pltpu.CompilerParams(
            dimension_semantics=("parallel", "parallel"), vmem_limit_bytes=VMEM_LIMIT),
        name="inproj",
    )(x, g, w, bd, qn, kn)


def _ssd_kernel(xbc_ref, z_ref, dt_ref, cw_ref, cb_ref, dtb_ref, alog_ref, dvec_ref, nw_ref,
                y_ref, st_ref, tail_ref, xbuf, hst, *, c, p0):
    ci = pl.program_id(1)

    @pl.when(ci == 0)
    def _():
        xbuf[0:SUBLANES, :] = jnp.zeros((SUBLANES, SSD_XBC), F32)
        hst[...] = jnp.zeros_like(hst)

    xbuf[SUBLANES:SUBLANES + c, :] = xbc_ref[0]
    conv = cb_ref[...]
    for j in range(4):
        conv = conv + xbuf[5 + j:5 + j + c, :] * cw_ref[j:j + 1, :]
    last = xbuf[c:c + SUBLANES, :]
    tail_ref[0] = last
    xbuf[0:SUBLANES, :] = last
    act = _silu(conv)
    xs = act[:, 0:384]
    bm = act[:, 384:640]
    cm = act[:, 640:896]

    rows = ci * c + _iota((c, 1), 0)
    dt = _softplus(dt_ref[0] + dtb_ref[...])
    dt = jnp.where(rows >= p0, dt, 0.0)
    a = dt * (-jnp.exp(alog_ref[...]))
    r_i = _iota((c, c), 0)
    c_i = _iota((c, c), 1)
    tri = r_i >= c_i
    cum = _dot_split_rhs(tri.astype(BF16), a, 3)
    cum_t = cum.T
    cum_last = cum[c - 1:c, :]
    e_cum = jnp.exp(cum)
    dec = jnp.exp(cum_last - cum) * dt
    e_last = jnp.exp(cum_last)

    lane = _iota((1, LANES), 1)
    lo_half = lane < HEAD_DIM
    rowp = _iota((LANES, 1), 0)
    top_half = rowp < HEAD_DIM
    cbs = [_dot_nt(cm[:, g * LANES:(g + 1) * LANES], bm[:, g * LANES:(g + 1) * LANES]) for g in range(2)]

    def col(m, h):
        return m[:, h:h + 1]

    ys = []
    for pr in range(3):
        ha, hb = 2 * pr, 2 * pr + 1
        ga, gb = ha // 3, hb // 3
        xp = xs[:, pr * LANES:(pr + 1) * LANES]
        xd = xp * jnp.where(lo_half, col(dt, ha), col(dt, hb))
        hpair = hst[pr * LANES:(pr + 1) * LANES, :]
        y_intra = []
        for h, g in ((ha, ga), (hb, gb)):
            seg = col(cum, h) - cum_t[h:h + 1, :]
            lm = jnp.exp(jnp.where(tri, seg, -jnp.inf))
            y_intra.append(_dot(cbs[g] * lm, xd))
        yi_a = _dot_nt(cm[:, ga * LANES:(ga + 1) * LANES], hpair)
        yi_b = yi_a if gb == ga else _dot_nt(cm[:, gb * LANES:(gb + 1) * LANES], hpair)
        y = jnp.where(lo_half, y_intra[0] + yi_a * col(e_cum, ha), y_intra[1] + yi_b * col(e_cum, hb))
        ys.append(y + xp * dvec_ref[:, pr * LANES:(pr + 1) * LANES])
        xe_t = (xp * jnp.where(lo_half, col(dec, ha), col(dec, hb))).T
        hn_a = _dot(xe_t, bm[:, ga * LANES:(ga + 1) * LANES])
        hn_b = hn_a if gb == ga else _dot(xe_t, bm[:, gb * LANES:(gb + 1) * LANES])
        keep = jnp.where(top_half, col(e_last, ha), col(e_last, hb))
        hst[pr * LANES:(pr + 1) * LANES, :] = hpair * keep + jnp.where(top_half, hn_a, hn_b)

    y = jnp.concatenate(ys, axis=-1) * _silu(z_ref[0])
    y_ref[0] = _rms(y, nw_ref[...]).astype(BF16)

    @pl.when(ci == pl.num_programs(1) - 1)
    def _():
        st_ref[0] = hst[...]


def _ssd(xbc, z, dt, cw, cb, dtb, alog, dvec, nw, *, c, p0):
    nb, rows, _ = xbc.shape

    def tile(width):
        return pl.BlockSpec((1, c, width), lambda b, i: (b, i, 0))

    def const(shape):
        return pl.BlockSpec(shape, lambda b, i: (0,) * len(shape))

    return pl.pallas_call(
        functools.partial(_ssd_kernel, c=c, p0=p0),
        grid=(nb, rows // c),
        in_specs=[tile(SSD_XBC), tile(SSD_WIDTH), tile(DT_COLS), const((4, SSD_XBC)), const((1, SSD_XBC)),
                  const((1, DT_COLS)), const((1, DT_COLS)), const((1, SSD_WIDTH)), const((1, SSD_WIDTH))],
        out_specs=[tile(SSD_WIDTH),
                   pl.BlockSpec((1, SSD_WIDTH, SSD_STATE), lambda b, i: (b, 0, 0)),
                   pl.BlockSpec((1, SUBLANES, SSD_XBC), lambda b, i: (b, 0, 0))],
        out_shape=[jax.ShapeDtypeStruct((nb, rows, SSD_WIDTH), BF16),
                   jax.ShapeDtypeStruct((nb, SSD_WIDTH, SSD_STATE), F32),
                   jax.ShapeDtypeStruct((nb, SUBLANES, SSD_XBC), F32)],
        scratch_shapes=[pltpu.VMEM((SUBLANES + c, SSD_XBC), F32), pltpu.VMEM((SSD_WIDTH, SSD_STATE), F32)],
        compiler_params=pltpu.CompilerParams(
            dimension_semantics=("arbitrary", "arbitrary"), vmem_limit_bytes=VMEM_LIMIT),
        name="ssd_scan",
    )(xbc, z, dt, cw, cb, dtb, alog, dvec, nw)


def _rg_gates(xc, wax_ref, bax_ref, lam_ref):
    ri = jax.nn.sigmoid(_dot(xc, wax_ref[...]) + bax_ref[...])
    r = ri[:, 0:RG_WIDTH]
    i = ri[:, RG_WIDTH:2 * RG_WIDTH]
    log_a = RG_C * r * (-_softplus(-lam_ref[...]))
    a = jnp.exp(log_a)
    u = jnp.sqrt(1.0 - jnp.exp(2.0 * log_a)) * (i * xc)
    return a, u


def _rg_kernel(xr_ref, gr_ref, cw_ref, cb_ref, wax_ref, bax_ref, lam_ref, nw_ref,
               y_ref, h_ref, tail_ref, xbuf, hprev, *, c, p0):
    ci = pl.program_id(1)

    @pl.when(ci == 0)
    def _():
        xbuf[0:SUBLANES, :] = jnp.zeros((SUBLANES, RG_WIDTH), F32)
        hprev[...] = jnp.zeros_like(hprev)

    xbuf[SUBLANES:SUBLANES + c, :] = xr_ref[0]
    xc = cb_ref[...]
    for j in range(4):
        xc = xc + xbuf[5 + j:5 + j + c, :] * cw_ref[j:j + 1, :]
    last = xbuf[c:c + SUBLANES, :]
    tail_ref[0] = last
    xbuf[0:SUBLANES, :] = last

    a, u = _rg_gates(xc, wax_ref, bax_ref, lam_ref)
    rowi = _iota((c, 1), 0)
    valid = (ci * c + rowi) >= p0
    a = jnp.where(valid, a, 1.0)
    u = jnp.where(valid, u, 0.0)
    d = 1
    while d < c:
        a_sh = jnp.where(rowi >= d, pltpu.roll(a, d, 0), 1.0)
        u_sh = jnp.where(rowi >= d, pltpu.roll(u, d, 0), 0.0)
        u = a * u_sh + u
        a = a * a_sh
        d *= 2
    h = a * hprev[...] + u
    h_last = h[c - 1:c, :]
    hprev[...] = h_last
    h_ref[0] = h_last
    y = h * _gelu_tanh(gr_ref[0])
    y_ref[0] = _rms(y, nw_ref[...]).astype(BF16)


def _rg(xr, gr, cw, cb, wax, bax, lam, nw, *, c, p0):
    nb, rows, _ = xr.shape

    def tile():
        return pl.BlockSpec((1, c, RG_WIDTH), lambda b, i: (b, i, 0))

    def const(shape):
        return pl.BlockSpec(shape, lambda b, i: (0,) * len(shape))

    return pl.pallas_call(
        functools.partial(_rg_kernel, c=c, p0=p0),
        grid=(nb, rows // c),
        in_specs=[tile(), tile(), const((4, RG_WIDTH)), const((1, RG_WIDTH)), const((RG_WIDTH, 2 * RG_WIDTH)),
                  const((1, 2 * RG_WIDTH)), const((1, RG_WIDTH)), const((1, RG_WIDTH))],
        out_specs=[tile(),
                   pl.BlockSpec((1, 1, RG_WIDTH), lambda b, i: (b, 0, 0)),
                   pl.BlockSpec((1, SUBLANES, RG_WIDTH), lambda b, i: (b, 0, 0))],
        out_shape=[jax.ShapeDtypeStruct((nb, rows, RG_WIDTH), BF16),
                   jax.ShapeDtypeStruct((nb, 1, RG_WIDTH), F32),
                   jax.ShapeDtypeStruct((nb, SUBLANES, RG_WIDTH), F32)],
        scratch_shapes=[pltpu.VMEM((SUBLANES + c, RG_WIDTH), F32), pltpu.VMEM((1, RG_WIDTH), F32)],
        compiler_params=pltpu.CompilerParams(dimension_semantics=("arbitrary", "arbitrary")),
        name="rglru_scan",
    )(xr, gr, cw, cb, wax, bax, lam, nw)


def _log2_one_minus_beta(zn):
    return jnp.minimum(zn, 0.0) - jnp.log2(1.0 + jnp.exp2(-jnp.abs(zn)))


def _attn_kernel(q_ref, k_ref, v_ref, kb_ref, o_ref, zraw, zbuf, lbuf, sbuf, tbuf, wbuf, cbuf, acc, qsb, umat,
                 *, tq, tk):
    qi = pl.program_id(2)
    per_tile = tq // tk
    top = per_tile * (qi + 1) - 1
    n = jnp.maximum(top + 1, 4)
    q = q_ref[0]
    lane = _iota((1, PAIR), 1)
    lo_half = lane < HEAD_DIM
    zero = jnp.zeros((), BF16)
    r_i = _iota((tq, tk), 0)
    c_i = _iota((tq, tk), 1)
    qsb[0] = jnp.where(lo_half, q, zero)
    qsb[1] = jnp.where(lo_half, zero, q)
    umat[...] = (_iota((tk, tk), 0) >= _iota((tk, tk), 1)).astype(BF16)

    cbuf[...] = jnp.zeros_like(cbuf)
    acc[...] = jnp.zeros_like(acc)

    def stage_qk(b):
        j = jnp.maximum(top - b, 0)
        ks = k_ref[0, pl.ds(pl.multiple_of(j * tk, tk), tk), :]
        for hh in range(2):
            zraw[hh] = _dot_nt(qsb[hh], ks)

    def stage_a(b, on_diagonal=False, maybe_masked=False):
        j = top - b
        if maybe_masked:
            dummy = j < 0
            j = jnp.maximum(j, 0)
        kbias = kb_ref[0, j]
        if maybe_masked:
            kbias = kbias + jnp.where(dummy, -NEG_BIG, 0.0)
        for hh in range(2):
            zn = zraw[hh] + kbias[hh:hh + 1, :]
            if on_diagonal:
                shift = tk * (per_tile - 1 - b)
                zn = zn + jnp.where(c_i + shift < r_i, 0.0, -NEG_BIG)
            zbuf[hh] = zn
            lbuf[hh] = _log2_one_minus_beta(zn).astype(BF16)

    def stage_b():
        for hh in range(2):
            incl = _dot(lbuf[hh], umat[...])
            sbuf[hh] = incl - zbuf[hh]
            tbuf[hh] = incl[:, 0:1]

    def stage_c1():
        for hh in range(2):
            c = cbuf[hh]
            wbuf[hh] = jnp.exp2(sbuf[hh] + c).astype(BF16)
            cbuf[hh] = c + tbuf[hh]

    def stage_c2(b):
        j = jnp.maximum(top - b, 0)
        vs = v_ref[0, pl.ds(pl.multiple_of(j * tk, tk), tk), :]
        acc[...] += (_dot(wbuf[0], jnp.where(lo_half, vs, zero))
                     + _dot(wbuf[1], jnp.where(lo_half, zero, vs)))

    def first_a(b):
        stage_a(b, on_diagonal=b < per_tile, maybe_masked=b >= per_tile)

    stage_qk(0)
    first_a(0)
    stage_qk(1)
    stage_b()
    first_a(1)
    stage_qk(2)
    stage_c1()
    stage_b()
    first_a(2)
    stage_qk(3)

    def body(i, carry):
        stage_c2(i - 4)
        stage_c1()
        stage_b()
        stage_a(i - 1)
        stage_qk(i)
        return carry

    lax.fori_loop(4, n, body, 0)
    stage_c2(n - 4)
    stage_c1()
    stage_b()
    stage_a(n - 1, maybe_masked=True)
    stage_c2(n - 3)
    stage_c1()
    stage_b()
    stage_c2(n - 2)
    stage_c1()
    stage_c2(n - 1)
    o_ref[0] = acc[...]


def _attn(qb, kb, vb, kbias, *, tq, tk):
    nb, rows, _ = qb.shape
    assert tq % tk == 0 and tq // tk <= 3 and rows % tq == 0
    nk = rows // tk
    return pl.pallas_call(
        functools.partial(_attn_kernel, tq=tq, tk=tk),
        grid=(nb, SB_WIDTH // PAIR, rows // tq),
        in_specs=[pl.BlockSpec((1, tq, PAIR), lambda b, p, i: (b, i, p)),
                  pl.BlockSpec((1, rows, PAIR), lambda b, p, i: (b, 0, p)),
                  pl.BlockSpec((1, rows, PAIR), lambda b, p, i: (b, 0, p)),
                  pl.BlockSpec((1, nk, SUBLANES, tk), lambda b, p, i: (p, 0, 0, 0))],
        out_specs=pl.BlockSpec((1, tq, PAIR), lambda b, p, i: (b, i, p)),
        out_shape=jax.ShapeDtypeStruct((nb, rows, SB_WIDTH), F32),
        scratch_shapes=[pltpu.VMEM((2, tq, tk), F32), pltpu.VMEM((2, tq, tk), F32), pltpu.VMEM((2, tq, tk), BF16),
                        pltpu.VMEM((2, tq, tk), F32), pltpu.VMEM((2, tq, 1), F32), pltpu.VMEM((2, tq, tk), BF16),
                        pltpu.VMEM((2, tq, 1), F32), pltpu.VMEM((tq, PAIR), F32),
                        pltpu.VMEM((2, tq, PAIR), BF16), pltpu.VMEM((tk, tk), BF16)],
        compiler_params=pltpu.CompilerParams(
            dimension_semantics=("arbitrary", "arbitrary", "arbitrary"), vmem_limit_bytes=VMEM_LIMIT),
        name="sb_attention",
    )(qb, kb, vb, kbias)


def _ffn_kernel(x_ref, yssd_ref, ysb_ref, yrg_ref, sbn_ref, wo_ref, n2_ref, wup_ref, cw_ref, cb_ref, wd_ref,
                out_ref, tail_ref, gbuf, *, tm, p0):
    i = pl.program_id(1)

    @pl.when(i == 0)
    def _():
        gbuf[0:SUBLANES, :] = jnp.zeros((SUBLANES, D_FF), F32)

    ysb = _rms(ysb_ref[0], sbn_ref[...]).astype(BF16)
    mix = jnp.concatenate([yssd_ref[0], ysb, yrg_ref[0]], axis=-1)
    xm = x_ref[0] + _dot(mix, wo_ref[...])
    h2 = _rms(xm, n2_ref[...])
    if p0 > 0:
        rows = i * tm + _iota((tm, 1), 0)
        h2 = jnp.where(rows >= p0, h2, 0.0)
    gu = _dot(h2.astype(BF16), wup_ref[...])
    gbuf[SUBLANES:SUBLANES + tm, :] = gu[:, 0:D_FF]
    conv = cb_ref[...]
    for j in range(3):
        conv = conv + gbuf[6 + j:6 + j + tm, :] * cw_ref[j:j + 1, :]
    last = gbuf[tm:tm + SUBLANES, :]
    tail_ref[0] = last
    gbuf[0:SUBLANES, :] = last
    act = (_silu(conv) * gu[:, D_FF:2 * D_FF]).astype(BF16)
    out_ref[0] = xm + _dot(act, wd_ref[...])


def _ffn(x, yssd, ysb, yrg, sbn, wo, n2, wup, cw, cb, wd, *, tm, p0):
    nb, rows, _ = x.shape

    def tile(width):
        return pl.BlockSpec((1, tm, width), lambda b, i: (b, i, 0))

    def const(shape):
        return pl.BlockSpec(shape, lambda b, i: (0,) * len(shape), pipeline_mode=pl.Buffered(1))

    return pl.pallas_call(
        functools.partial(_ffn_kernel, tm=tm, p0=p0),
        grid=(nb, rows // tm),
        in_specs=[tile(D_MODEL), tile(SSD_WIDTH), tile(SB_WIDTH), tile(RG_WIDTH), const((1, SB_WIDTH)),
                  const((D_MODEL, D_MODEL)), const((1, D_MODEL)), const((D_MODEL, 2 * D_FF)),
                  const((3, D_FF)), const((1, D_FF)), const((D_FF, D_MODEL))],
        out_specs=[tile(D_MODEL), pl.BlockSpec((1, SUBLANES, D_FF), lambda b, i: (b, 0, 0))],
        out_shape=[jax.ShapeDtypeStruct((nb, rows, D_MODEL), F32),
                   jax.ShapeDtypeStruct((nb, SUBLANES, D_FF), F32)],
        scratch_shapes=[pltpu.VMEM((SUBLANES + tm, D_FF), F32)],
        compiler_params=pltpu.CompilerParams(
            dimension_semantics=("arbitrary", "arbitrary"), vmem_limit_bytes=VMEM_LIMIT),
        name="outproj_ffn",
    )(x, yssd, ysb, yrg, sbn, wo, n2, wup, cw, cb, wd)


def _sample_mix_kernel(z_ref, xbc_ref, dt_ref, sbuf_ref, st_ref, xr_ref, gr_ref, rbuf_ref, rh_ref,
                       scw_ref, scb_ref, dtb_ref, alog_ref, dvec_ref, snw_ref,
                       rcw_ref, rcb_ref, wax_ref, bax_ref, lam_ref, rnw_ref,
                       yssd_ref, st_out_ref, sbuf_out_ref, yrg_ref, rh_out_ref, rbuf_out_ref,
                       yscr, *, nseq):
    xbc = xbc_ref[...]
    conv = scb_ref[...] + xbc * scw_ref[3:4, :]
    for j in range(3):
        conv = conv + sbuf_ref[j] * scw_ref[j:j + 1, :]
    sbuf_out_ref[0] = sbuf_ref[1]
    sbuf_out_ref[1] = sbuf_ref[2]
    sbuf_out_ref[2] = xbc
    act = _silu(conv)
    xs = act[:, 0:384]
    bm = act[:, 384:640]
    cm = act[:, 640:896]
    dt = _softplus(dt_ref[...] + dtb_ref[...])
    da = jnp.exp(dt * (-jnp.exp(alog_ref[...])))

    lane = _iota((1, LANES), 1)
    lo_half = lane < HEAD_DIM
    rowp = _iota((LANES, 1), 0)
    top_half = rowp < HEAD_DIM
    pad_rows = LANES - nseq
    yscr[...] = jnp.zeros_like(yscr)
    for pr in range(3):
        ha, hb = 2 * pr, 2 * pr + 1
        ga, gb = ha // 3, hb // 3
        xp = xs[:, pr * LANES:(pr + 1) * LANES]
        xd = xp * jnp.where(lo_half, dt[:, ha:ha + 1], dt[:, hb:hb + 1])
        xd_t = jnp.concatenate([xd, jnp.zeros((pad_rows, LANES), F32)], axis=0).T
        for b in range(nseq):
            hpair = st_ref[b, pr * LANES:(pr + 1) * LANES, :]
            keep = jnp.where(top_half, da[b:b + 1, ha:ha + 1], da[b:b + 1, hb:hb + 1])
            brow = jnp.where(top_half, bm[b:b + 1, ga * LANES:(ga + 1) * LANES],
                             bm[b:b + 1, gb * LANES:(gb + 1) * LANES])
            hnew = hpair * keep + xd_t[:, b:b + 1] * brow
            st_out_ref[b, pr * LANES:(pr + 1) * LANES, :] = hnew
            crow = jnp.where(top_half, cm[b:b + 1, ga * LANES:(ga + 1) * LANES],
                             cm[b:b + 1, gb * LANES:(gb + 1) * LANES])
            ycol = jnp.sum(hnew * crow, axis=1, keepdims=True)
            yscr[pr, :, b:b + 1] = ycol
    ys = []
    for pr in range(3):
        ys.append(yscr[pr].T[0:nseq, :])
    y = jnp.concatenate(ys, axis=-1) + xs * dvec_ref[...]
    y = y * _silu(z_ref[...])
    yssd_ref[...] = _rms(y, snw_ref[...]).astype(BF16)

    xr = xr_ref[...]
    xc = rcb_ref[...] + xr * rcw_ref[3:4, :]
    for j in range(3):
        xc = xc + rbuf_ref[j] * rcw_ref[j:j + 1, :]
    rbuf_out_ref[0] = rbuf_ref[1]
    rbuf_out_ref[1] = rbuf_ref[2]
    rbuf_out_ref[2] = xr
    a, u = _rg_gates(xc, wax_ref, bax_ref, lam_ref)
    h = a * rh_ref[...] + u
    rh_out_ref[...] = h
    yrg_ref[...] = _rms(h * _gelu_tanh(gr_ref[...]), rnw_ref[...]).astype(BF16)


def _sample_mix(z, xbc, dt, sbuf, st, xr, gr, rbuf, rh, scw, scb, dtb, alog, dvec, snw,
                rcw, rcb, wax, bax, lam, rnw):
    nseq = z.shape[0]
    return pl.pallas_call(
        functools.partial(_sample_mix_kernel, nseq=nseq),
        out_shape=[jax.ShapeDtypeStruct((nseq, SSD_WIDTH), BF16),
                   jax.ShapeDtypeStruct((nseq, SSD_WIDTH, SSD_STATE), F32),
                   jax.ShapeDtypeStruct((3, nseq, SSD_XBC), F32),
                   jax.ShapeDtypeStruct((nseq, RG_WIDTH), BF16),
                   jax.ShapeDtypeStruct((nseq, RG_WIDTH), F32),
                   jax.ShapeDtypeStruct((3, nseq, RG_WIDTH), F32)],
        scratch_shapes=[pltpu.VMEM((3, LANES, LANES), F32)],
        compiler_params=pltpu.CompilerParams(vmem_limit_bytes=VMEM_LIMIT),
        name="sample_mixers",
    )(z, xbc, dt, sbuf, st, xr, gr, rbuf, rh, scw, scb, dtb, alog, dvec, snw, rcw, rcb, wax, bax, lam, rnw)


def _sample_attn_kernel(pt_ref, q_ref, bias_ref, *refs, npg):
    del pt_ref
    k_refs = refs[0:npg]
    v_refs = refs[npg:2 * npg]
    o_ref = refs[2 * npg]
    carry_ref, acc_ref = refs[2 * npg + 1:]
    g = pl.program_id(1)

    @pl.when(g == 0)
    def _():
        carry_ref[...] = jnp.zeros_like(carry_ref)
        acc_ref[...] = jnp.zeros_like(acc_ref)

    qb = q_ref[0]
    r_i = _iota((PAGE, PAGE), 0)
    c_i = _iota((PAGE, PAGE), 1)
    incl_mat = (r_i >= c_i).astype(BF16)
    pad = jnp.zeros((SUBLANES - SB_HEADS, PAGE), F32)
    zs, ls = [], []
    for r in range(npg):
        prod = k_refs[r][0] * qb
        rows = [jnp.sum(prod[h * HEAD_DIM:(h + 1) * HEAD_DIM, :], axis=0, keepdims=True)
                for h in range(SB_HEADS)]
        zn = jnp.concatenate(rows + [pad], axis=0) + bias_ref[...]
        zs.append(zn)
        ls.append(_log2_one_minus_beta(zn).astype(BF16))
    carry = carry_ref[...]
    for r in range(npg):
        incl = _dot(ls[r], incl_mat)
        w = jnp.exp2(incl - zs[r] + carry)
        for h in range(SB_HEADS):
            hs = slice(h * HEAD_DIM, (h + 1) * HEAD_DIM)
            acc_ref[hs, :] += jnp.broadcast_to(w[h:h + 1, :], (HEAD_DIM, PAGE)) * v_refs[r][0, hs, :]
        carry = carry + incl[:, 0:1]
    carry_ref[...] = carry

    @pl.when(g == pl.num_programs(1) - 1)
    def _():
        o_ref[0] = jnp.sum(acc_ref[...], axis=1, keepdims=True)


def _sample_attn(pt_flat, q3, bias8, ck, cv, *, nseq, n_pages, npg):
    steps = n_pages // npg

    def page_spec(r):
        def imap(b, g, pt):
            return (pt[b * n_pages + (n_pages - 1 - (g * npg + r))], 0, 0)
        return pl.BlockSpec((1, SB_WIDTH, PAGE), imap)

    grid_spec = pltpu.PrefetchScalarGridSpec(
        num_scalar_prefetch=1,
        grid=(nseq, steps),
        in_specs=[pl.BlockSpec((1, SB_WIDTH, PAGE), lambda b, g, pt: (b, 0, 0)),
                  pl.BlockSpec((SUBLANES, PAGE), lambda b, g, pt: (0, 0))]
                 + [page_spec(r) for r in range(npg)] + [page_spec(r) for r in range(npg)],
        out_specs=pl.BlockSpec((1, SB_WIDTH, 1), lambda b, g, pt: (b, 0, 0)),
        scratch_shapes=[pltpu.VMEM((SUBLANES, 1), F32), pltpu.VMEM((SB_WIDTH, PAGE), F32)],
    )
    return pl.pallas_call(
        functools.partial(_sample_attn_kernel, npg=npg),
        grid_spec=grid_spec,
        out_shape=jax.ShapeDtypeStruct((nseq, SB_WIDTH, 1), F32),
        compiler_params=pltpu.CompilerParams(dimension_semantics=("arbitrary", "arbitrary")),
        name="sample_attention",
    )(pt_flat, q3, bias8, *([ck] * npg), *([cv] * npg))


def _sample_ffn_kernel(x_ref, yssd_ref, ysb_ref, yrg_ref, fbuf_ref, sbn_ref, wo_ref, n2_ref, wup_ref,
                       cw_ref, cb_ref, wd_ref, out_ref, fbuf_out_ref):
    ysb = _rms(ysb_ref[...], sbn_ref[...]).astype(BF16)
    mix = jnp.concatenate([yssd_ref[...], ysb, yrg_ref[...]], axis=-1)
    xm = x_ref[...] + _dot(mix, wo_ref[...])
    h2 = _rms(xm, n2_ref[...])
    gu = _dot(h2.astype(BF16), wup_ref[...])
    g = gu[:, 0:D_FF]
    conv = cb_ref[...] + fbuf_ref[0] * cw_ref[0:1, :] + fbuf_ref[1] * cw_ref[1:2, :] + g * cw_ref[2:3, :]
    fbuf_out_ref[0] = fbuf_ref[1]
    fbuf_out_ref[1] = g
    act = (_silu(conv) * gu[:, D_FF:2 * D_FF]).astype(BF16)
    out_ref[...] = xm + _dot(act, wd_ref[...])


def _sample_ffn(x, yssd, ysb, yrg, fbuf, sbn, wo, n2, wup, cw, cb, wd):
    nseq = x.shape[0]
    return pl.pallas_call(
        _sample_ffn_kernel,
        out_shape=[jax.ShapeDtypeStruct((nseq, D_MODEL), F32),
                   jax.ShapeDtypeStruct((2, nseq, D_FF), F32)],
        compiler_params=pltpu.CompilerParams(vmem_limit_bytes=VMEM_LIMIT),
        name="sample_outproj_ffn",
    )(x, yssd, ysb, yrg, fbuf, sbn, wo, n2, wup, cw, cb, wd)


def _pad_cols(a, width):
    return jnp.pad(a, ((0, 0), (0, width - a.shape[-1])))


def _block_diag(w):
    nblk, bi, bj = w.shape
    out = jnp.zeros((nblk * bi, nblk * bj), w.dtype)
    for h in range(nblk):
        out = out.at[h * bi:(h + 1) * bi, h * bj:(h + 1) * bj].set(w[h])
    return out


def kernel(x_prompt, x_sample, cache_k, cache_v, page_table, state_ssm, state_ssm_conv, state_rg, state_rg_conv, state_ffn_conv, meta_tokens, norm1, w_in, ssd_conv_w, ssd_conv_b, ssd_dt_bias, ssd_a_log, ssd_d, ssd_norm, q_norm, k_norm, sb_bias, sb_out_norm, rg_conv_w, rg_conv_b, rg_wa, rg_ba, rg_wx, rg_bx, rg_lambda, rg_out_norm, w_out, norm2, w_up, ffn_conv_w, ffn_conv_b, w_down):
    depth = w_in.shape[0]
    nb, seq, _ = x_prompt.shape
    nseq = x_sample.shape[0]
    n_pool = cache_k.shape[1]
    n_pages = page_table.shape[1]
    t_real = N_META + seq
    t_pad = -(-t_real // ATT_TQ) * ATT_TQ
    p0 = t_pad - t_real
    nk = t_pad // ATT_TK

    meta = jnp.broadcast_to(meta_tokens[None], (nb, N_META, D_MODEL))
    xp = jnp.concatenate([jnp.zeros((nb, p0, D_MODEL), F32), meta, x_prompt], axis=1)
    xs = x_sample.reshape(1, nseq, D_MODEL)

    head_of = np.arange(SB_WIDTH) // HEAD_DIM
    bd = jnp.asarray(head_of[:, None] == head_of[None, :], BF16)
    key_valid = jnp.arange(t_pad) >= p0
    ck = cache_k.transpose(0, 1, 3, 4, 2).reshape(depth * n_pool, SB_WIDTH, PAGE)
    cv = cache_v.transpose(0, 1, 3, 4, 2).reshape(depth * n_pool, SB_WIDTH, PAGE)

    outs_p, outs_s = [], []
    for l in range(depth):
        wz, wxbc, wdt, wq, wk, wv, wxr, wgr = jnp.split(
            w_in[l].T, np.cumsum([384, 896, 6, 384, 384, 384, 256])[:].tolist(), axis=0)
        wdt = jnp.pad(wdt, ((0, DT_COLS - wdt.shape[0]), (0, 0)))
        w1 = jnp.concatenate([wz, wxbc, wq, wk, wv, wxr, wgr, wdt], axis=0).astype(BF16)
        g1 = norm1[l][None]
        qn = jnp.tile(q_norm[l], SB_HEADS)[None]
        kn = jnp.tile(k_norm[l], SB_HEADS)[None]
        dtb = _pad_cols(ssd_dt_bias[l][None], DT_COLS)
        alog = _pad_cols(ssd_a_log[l][None], DT_COLS)
        dvec = jnp.repeat(ssd_d[l], HEAD_DIM)[None]
        snw = ssd_norm[l][None]
        scw, scb = ssd_conv_w[l], ssd_conv_b[l][None]
        rcw, rcb = rg_conv_w[l], rg_conv_b[l][None]
        wax = jnp.concatenate([_block_diag(rg_wa[l]), _block_diag(rg_wx[l])], axis=1)
        bax = jnp.concatenate([rg_ba[l].reshape(1, RG_WIDTH), rg_bx[l].reshape(1, RG_WIDTH)], axis=1)
        lam = rg_lambda[l][None]
        rnw = rg_out_norm[l][None]
        sbn = sb_out_norm[l][None]
        wo = w_out[l].astype(BF16)
        n2 = norm2[l][None]
        wup = w_up[l].astype(BF16)
        fcw, fcb = ffn_conv_w[l], ffn_conv_b[l][None]
        wd = w_down[l].astype(BF16)

        z, xbc, dt, qb, kb, vb, kf, vf, xr, gr = _inproj(xp, g1, w1, bd, qn, kn, tm=ROW_TILE, p0=p0)
        yssd, ssm_p, sconv_p = _ssd(xbc, z, dt, scw, scb, dtb, alog, dvec, snw, c=CHUNK, p0=p0)
        yrg, rgh_p, rconv_p = _rg(xr, gr, rcw, rcb, wax, bax, lam, rnw, c=CHUNK, p0=p0)
        neg_bias = -LOG2E * sb_bias[l]
        kbias = jnp.where(key_valid[None, :], neg_bias[:, None], -NEG_BIG)
        kbias = kbias.reshape(3, 2, nk, ATT_TK).transpose(0, 2, 1, 3)
        kbias = jnp.pad(kbias, ((0, 0), (0, 0), (0, SUBLANES - 2), (0, 0)))
        ysb = _attn(qb, kb, vb, kbias, tq=ATT_TQ, tk=ATT_TK)
        xp, fconv_p = _ffn(xp, yssd, ysb, yrg, sbn, wo, n2, wup, fcw, fcb, wd, tm=ROW_TILE, p0=p0)
        outs_p.append((kf[:, p0:].reshape(nb, t_real, SB_HEADS, HEAD_DIM),
                       vf[:, p0:].reshape(nb, t_real, SB_HEADS, HEAD_DIM),
                       ssm_p.reshape(nb, SSD_HEADS, HEAD_DIM, SSD_STATE),
                       sconv_p[:, SUBLANES - 3:],
                       rgh_p.reshape(nb, RG_WIDTH),
                       rconv_p[:, SUBLANES - 3:],
                       fconv_p[:, SUBLANES - 2:]))

        z, xbc, dt, qb, kb, vb, kf, vf, xr, gr = _inproj(xs, g1, w1, bd, qn, kn, tm=nseq, p0=0)
        yssd, ssm_s, sconv_s, yrg, rgh_s, rconv_s = _sample_mix(
            z[0], xbc[0], dt[0], state_ssm_conv[l].transpose(1, 0, 2),
            state_ssm[l].reshape(nseq, SSD_WIDTH, SSD_STATE), xr[0], gr[0],
            state_rg_conv[l].transpose(1, 0, 2), state_rg[l],
            scw, scb, dtb, alog, dvec, snw, rcw, rcb, wax, bax, lam, rnw)
        bias8 = jnp.broadcast_to(jnp.pad(neg_bias, (0, SUBLANES - SB_HEADS))[:, None], (SUBLANES, PAGE))
        pt_flat = (page_table + l * n_pool).reshape(-1).astype(jnp.int32)
        q_lanes = jnp.broadcast_to(qb[0].astype(F32)[:, :, None], (nseq, SB_WIDTH, PAGE))
        ysb = _sample_attn(pt_flat, q_lanes, bias8, ck, cv,
                           nseq=nseq, n_pages=n_pages, npg=PAGES_PER_STEP)
        xs2, fconv_s = _sample_ffn(xs[0], yssd, ysb.reshape(nseq, SB_WIDTH), yrg,
                                   state_ffn_conv[l].transpose(1, 0, 2), sbn, wo, n2, wup, fcw, fcb, wd)
        xs = xs2[None]
        outs_s.append((kf.reshape(nseq, 1, SB_HEADS, HEAD_DIM),
                       vf.reshape(nseq, 1, SB_HEADS, HEAD_DIM),
                       ssm_s.reshape(nseq, SSD_HEADS, HEAD_DIM, SSD_STATE),
                       sconv_s.transpose(1, 0, 2),
                       rgh_s,
                       rconv_s.transpose(1, 0, 2),
                       fconv_s.transpose(1, 0, 2)))

    def stk(outs, i):
        return jnp.stack([o[i] for o in outs], axis=0)

    y_prompt = xp[:, p0 + N_META:]
    y_sample = xs.reshape(nseq, 1, D_MODEL)
    return (y_prompt, y_sample,
            stk(outs_p, 0), stk(outs_p, 1), stk(outs_p, 2), stk(outs_p, 3), stk(outs_p, 4), stk(outs_p, 5), stk(outs_p, 6),
            stk(outs_s, 0), stk(outs_s, 1), stk(outs_s, 2), stk(outs_s, 3), stk(outs_s, 4), stk(outs_s, 5), stk(outs_s, 6))
```

```python
import functools

import jax
import jax.numpy as jnp
import numpy as np
from jax import lax
from jax.experimental import pallas as pl
from jax.experimental.pallas import tpu as pltpu

F32 = jnp.float32
BF16 = jnp.bfloat16

D_MODEL = 1024
N_META = 16
HEAD_DIM = 64
SSD_WIDTH = 384
SSD_HEADS = 6
SSD_STATE = 128
SSD_XBC = 896
SB_WIDTH = 384
SB_HEADS = 6
RG_WIDTH = 256
RG_C = 8.0
D_FF = 2816
EPS = 1e-6

LANES = 128
SUBLANES = 8
PAIR = 2 * HEAD_DIM
DT_COLS = LANES
IN_COLS_PAD = 2 * SSD_WIDTH + 512 + 3 * SB_WIDTH + 2 * RG_WIDTH + DT_COLS
NEG_BIG = -1e30
LOG2E = 1.4426950408889634
LOGIT2_MAX = 126.0

ROW_TILE = 256
CHUNK = 128
ATT_TQ = 256
ATT_TK = 256
PAGE = 128
PAGES_PER_STEP = 16
VMEM_LIMIT = 56 * 1024 * 1024

NT_DIMS = (((1,), (1,)), ((), ()))


def _dot(a, b):
    return jnp.dot(a, b, preferred_element_type=F32)


def _dot_nt(a, b):
    return lax.dot_general(a, b, NT_DIMS, preferred_element_type=F32)


def _split_bf16(x, n):
    parts = []
    r = x
    for _ in range(n):
        p = r.astype(BF16)
        parts.append(p)
        r = r - p.astype(F32)
    return parts


def _dot_split_lhs(x, m, n):
    out = None
    for p in _split_bf16(x, n):
        t = _dot(p, m)
        out = t if out is None else out + t
    return out


def _dot_split_rhs(m, x, n):
    out = None
    for p in _split_bf16(x, n):
        t = _dot(m, p)
        out = t if out is None else out + t
    return out


def _rms(x, g):
    return x * lax.rsqrt(jnp.mean(x * x, axis=-1, keepdims=True) + EPS) * g


def _softplus(x):
    return jnp.maximum(x, 0.0) + jnp.log(1.0 + jnp.exp(-jnp.abs(x)))


def _silu(x):
    return x * jax.nn.sigmoid(x)


def _gelu_tanh(x):
    return 0.5 * x * (1.0 + jnp.tanh(0.7978845608028654 * (x + 0.044715 * (x * x * x))))


def _iota(shape, dim):
    return lax.broadcasted_iota(jnp.int32, shape, dim)


def _inproj_kernel(x_ref, g_ref, w_ref, bd_ref, qn_ref, kn_ref,
                   z_ref, xbc_ref, dt_ref, qb_ref, kb_ref, vb_ref, k_ref, v_ref, xr_ref, gr_ref,
                   *, tm, p0):
    x = x_ref[0]
    h = _rms(x, g_ref[...])
    if p0 > 0:
        rows = pl.program_id(1) * tm + _iota((tm, 1), 0)
        h = jnp.where(rows >= p0, h, 0.0)
    proj = _dot_nt(h.astype(BF16), w_ref[...])
    z_ref[0] = proj[:, 0:384]
    xbc_ref[0] = proj[:, 384:1280]
    q = proj[:, 1280:1664]
    k = proj[:, 1664:2048]
    v = proj[:, 2048:2432]
    xr_ref[0] = proj[:, 2432:2688]
    gr_ref[0] = proj[:, 2688:2944]
    dt_ref[0] = proj[:, 2944:3072]
    bd = bd_ref[...]
    q_ms = _dot_split_lhs(q * q, bd, 2) * (1.0 / HEAD_DIM)
    k_ms = _dot_split_lhs(k * k, bd, 2) * (1.0 / HEAD_DIM)
    qn = q * lax.rsqrt(q_ms + EPS) * qn_ref[...]
    kn = k * lax.rsqrt(k_ms + EPS) * kn_ref[...]
    qb_ref[0] = (qn * (LOG2E * HEAD_DIM ** -0.5)).astype(BF16)
    kb_ref[0] = kn.astype(BF16)
    vb_ref[0] = v.astype(BF16)
    k_ref[0] = kn
    v_ref[0] = v


def _inproj(x, g, w, bd, qn, kn, *, tm, p0):
    nb, rows, _ = x.shape
    grid = (nb, rows // tm)

    def tile(width):
        return pl.BlockSpec((1, tm, width), lambda b, i: (b, i, 0))

    def const(shape):
        return pl.BlockSpec(shape, lambda b, i: (0,) * len(shape), pipeline_mode=pl.Buffered(1))

    widths = (384, 896, DT_COLS, 384, 384, 384, 384, 384, 256, 256)
    dtypes = (F32, F32, F32, BF16, BF16, BF16, F32, F32, F32, F32)
    return pl.pallas_call(
        functools.partial(_inproj_kernel, tm=tm, p0=p0),
        grid=grid,
        in_specs=[tile(D_MODEL), const((1, D_MODEL)), const((IN_COLS_PAD, D_MODEL)),
                  const((384, 384)), const((1, 384)), const((1, 384))],
        out_specs=[tile(wd) for wd in widths],
        out_shape=[jax.ShapeDtypeStruct((nb, rows, wd), dt) for wd, dt in zip(widths, dtypes)],
        compiler_params=pltpu.CompilerParams(
            dimension_semantics=("parallel", "parallel"), vmem_limit_bytes=VMEM_LIMIT),
        name="inproj",
    )(x, g, w, bd, qn, kn)


def _ssd_kernel(xbc_ref, z_ref, dt_ref, cw_ref, cb_ref, dtb_ref, alog_ref, dvec_ref, nw_ref,
                y_ref, st_ref, tail_ref, xbuf, hst, *, nb, c, p0):
    ci = pl.program_id(0)

    @pl.when(ci == 0)
    def _():
        xbuf[:, 0:SUBLANES, :] = jnp.zeros((nb, SUBLANES, SSD_XBC), F32)
        hst[...] = jnp.zeros_like(hst)

    for bi in range(nb):
        _ssd_chunk(bi, ci, xbc_ref, z_ref, dt_ref, cw_ref, cb_ref, dtb_ref, alog_ref, dvec_ref, nw_ref,
                   y_ref, tail_ref, xbuf, hst, c=c, p0=p0)

    @pl.when(ci == pl.num_programs(0) - 1)
    def _():
        st_ref[...] = hst[...]


def _ssd_chunk(bi, ci, xbc_ref, z_ref, dt_ref, cw_ref, cb_ref, dtb_ref, alog_ref, dvec_ref, nw_ref,
               y_ref, tail_ref, xbuf, hst, *, c, p0):
    xbuf[bi, SUBLANES:SUBLANES + c, :] = xbc_ref[bi]
    conv = cb_ref[...]
    for j in range(4):
        conv = conv + xbuf[bi, 5 + j:5 + j + c, :] * cw_ref[j:j + 1, :]
    last = xbuf[bi, c:c + SUBLANES, :]
    tail_ref[bi] = last
    xbuf[bi, 0:SUBLANES, :] = last
    act = _silu(conv)
    xs = act[:, 0:384]
    bm = act[:, 384:640]
    cm = act[:, 640:896]

    rows = ci * c + _iota((c, 1), 0)
    dt = _softplus(dt_ref[bi] + dtb_ref[...])
    dt = jnp.where(rows >= p0, dt, 0.0)
    a = dt * (-jnp.exp(alog_ref[...]))
    r_i = _iota((c, c), 0)
    c_i = _iota((c, c), 1)
    tri = r_i >= c_i
    cum = _dot_split_rhs(tri.astype(BF16), a, 3)
    cum_t = cum.T
    cum_last = cum[c - 1:c, :]
    e_cum = jnp.exp(cum)
    dec = jnp.exp(cum_last - cum) * dt
    e_last = jnp.exp(cum_last)

    lane = _iota((1, LANES), 1)
    lo_half = lane < HEAD_DIM
    rowp = _iota((LANES, 1), 0)
    top_half = rowp < HEAD_DIM
    cbs = [_dot_nt(cm[:, g * LANES:(g + 1) * LANES], bm[:, g * LANES:(g + 1) * LANES]) for g in range(2)]

    def col(m, h):
        return m[:, h:h + 1]

    ys = []
    for pr in range(3):
        ha, hb = 2 * pr, 2 * pr + 1
        ga, gb = ha // 3, hb // 3
        xp = xs[:, pr * LANES:(pr + 1) * LANES]
        xd = xp * jnp.where(lo_half, col(dt, ha), col(dt, hb))
        hpair = hst[bi, pr * LANES:(pr + 1) * LANES, :]
        y_intra = []
        for h, g in ((ha, ga), (hb, gb)):
            seg = col(cum, h) - cum_t[h:h + 1, :]
            lm = jnp.exp(jnp.where(tri, seg, -jnp.inf))
            y_intra.append(_dot(cbs[g] * lm, xd))
        yi_a = _dot_nt(cm[:, ga * LANES:(ga + 1) * LANES], hpair)
        yi_b = yi_a if gb == ga else _dot_nt(cm[:, gb * LANES:(gb + 1) * LANES], hpair)
        y = jnp.where(lo_half, y_intra[0] + yi_a * col(e_cum, ha), y_intra[1] + yi_b * col(e_cum, hb))
        ys.append(y + xp * dvec_ref[:, pr * LANES:(pr + 1) * LANES])
        xe_t = (xp * jnp.where(lo_half, col(dec, ha), col(dec, hb))).T
        hn_a = _dot(xe_t, bm[:, ga * LANES:(ga + 1) * LANES])
        hn_b = hn_a if gb == ga else _dot(xe_t, bm[:, gb * LANES:(gb + 1) * LANES])
        keep = jnp.where(top_half, col(e_last, ha), col(e_last, hb))
        hst[bi, pr * LANES:(pr + 1) * LANES, :] = hpair * keep + jnp.where(top_half, hn_a, hn_b)

    y = jnp.concatenate(ys, axis=-1) * _silu(z_ref[bi])
    y_ref[bi] = _rms(y, nw_ref[...]).astype(BF16)


def _ssd(xbc, z, dt, cw, cb, dtb, alog, dvec, nw, *, c, p0):
    nb, rows, _ = xbc.shape

    def tile(width):
        return pl.BlockSpec((nb, c, width), lambda i: (0, i, 0))

    def const(shape):
        return pl.BlockSpec(shape, lambda i: (0,) * len(shape))

    return pl.pallas_call(
        functools.partial(_ssd_kernel, nb=nb, c=c, p0=p0),
        grid=(rows // c,),
        in_specs=[tile(SSD_XBC), tile(SSD_WIDTH), tile(DT_COLS), const((4, SSD_XBC)), const((1, SSD_XBC)),
                  const((1, DT_COLS)), const((1, DT_COLS)), const((1, SSD_WIDTH)), const((1, SSD_WIDTH))],
        out_specs=[tile(SSD_WIDTH), const((nb, SSD_WIDTH, SSD_STATE)), const((nb, SUBLANES, SSD_XBC))],
        out_shape=[jax.ShapeDtypeStruct((nb, rows, SSD_WIDTH), BF16),
                   jax.ShapeDtypeStruct((nb, SSD_WIDTH, SSD_STATE), F32),
                   jax.ShapeDtypeStruct((nb, SUBLANES, SSD_XBC), F32)],
        scratch_shapes=[pltpu.VMEM((nb, SUBLANES + c, SSD_XBC), F32),
                        pltpu.VMEM((nb, SSD_WIDTH, SSD_STATE), F32)],
        compiler_params=pltpu.CompilerParams(
            dimension_semantics=("arbitrary",), vmem_limit_bytes=VMEM_LIMIT),
        name="ssd_scan",
    )(xbc, z, dt, cw, cb, dtb, alog, dvec, nw)


def _rg_gates(xc, wax_ref, bax_ref, lam_ref):
    ri = jax.nn.sigmoid(_dot(xc, wax_ref[...]) + bax_ref[...])
    r = ri[:, 0:RG_WIDTH]
    i = ri[:, RG_WIDTH:2 * RG_WIDTH]
    log_a = RG_C * r * (-_softplus(-lam_ref[...]))
    a = jnp.exp(log_a)
    u = jnp.sqrt(1.0 - jnp.exp(2.0 * log_a)) * (i * xc)
    return a, u


def _rg_kernel(xr_ref, gr_ref, cw_ref, cb_ref, wax_ref, bax_ref, lam_ref, nw_ref,
               y_ref, h_ref, tail_ref, xbuf, hprev, *, nb, c, p0):
    ci = pl.program_id(0)

    @pl.when(ci == 0)
    def _():
        xbuf[:, 0:SUBLANES, :] = jnp.zeros((nb, SUBLANES, RG_WIDTH), F32)
        hprev[...] = jnp.zeros_like(hprev)

    for bi in range(nb):
        _rg_chunk(bi, ci, xr_ref, gr_ref, cw_ref, cb_ref, wax_ref, bax_ref, lam_ref, nw_ref,
                  y_ref, h_ref, tail_ref, xbuf, hprev, c=c, p0=p0)


def _rg_chunk(bi, ci, xr_ref, gr_ref, cw_ref, cb_ref, wax_ref, bax_ref, lam_ref, nw_ref,
              y_ref, h_ref, tail_ref, xbuf, hprev, *, c, p0):
    xbuf[bi, SUBLANES:SUBLANES + c, :] = xr_ref[bi]
    xc = cb_ref[...]
    for j in range(4):
        xc = xc + xbuf[bi, 5 + j:5 + j + c, :] * cw_ref[j:j + 1, :]
    last = xbuf[bi, c:c + SUBLANES, :]
    tail_ref[bi] = last
    xbuf[bi, 0:SUBLANES, :] = last

    a, u = _rg_gates(xc, wax_ref, bax_ref, lam_ref)
    rowi = _iota((c, 1), 0)
    valid = (ci * c + rowi) >= p0
    a = jnp.where(valid, a, 1.0)
    u = jnp.where(valid, u, 0.0)
    d = 1
    while d < c:
        a_sh = jnp.where(rowi >= d, pltpu.roll(a, d, 0), 1.0)
        u_sh = jnp.where(rowi >= d, pltpu.roll(u, d, 0), 0.0)
        u = a * u_sh + u
        a = a * a_sh
        d *= 2
    h = a * hprev[bi] + u
    h_last = h[c - 1:c, :]
    hprev[bi] = h_last
    h_ref[bi] = h_last
    y = h * _gelu_tanh(gr_ref[bi])
    y_ref[bi] = _rms(y, nw_ref[...]).astype(BF16)


def _rg(xr, gr, cw, cb, wax, bax, lam, nw, *, c, p0):
    nb, rows, _ = xr.shape

    def tile():
        return pl.BlockSpec((nb, c, RG_WIDTH), lambda i: (0, i, 0))

    def const(shape):
        return pl.BlockSpec(shape, lambda i: (0,) * len(shape))

    return pl.pallas_call(
        functools.partial(_rg_kernel, nb=nb, c=c, p0=p0),
        grid=(rows // c,),
        in_specs=[tile(), tile(), const((4, RG_WIDTH)), const((1, RG_WIDTH)), const((RG_WIDTH, 2 * RG_WIDTH)),
                  const((1, 2 * RG_WIDTH)), const((1, RG_WIDTH)), const((1, RG_WIDTH))],
        out_specs=[tile(), const((nb, 1, RG_WIDTH)), const((nb, SUBLANES, RG_WIDTH))],
        out_shape=[jax.ShapeDtypeStruct((nb, rows, RG_WIDTH), BF16),
                   jax.ShapeDtypeStruct((nb, 1, RG_WIDTH), F32),
                   jax.ShapeDtypeStruct((nb, SUBLANES, RG_WIDTH), F32)],
        scratch_shapes=[pltpu.VMEM((nb, SUBLANES + c, RG_WIDTH), F32), pltpu.VMEM((nb, 1, RG_WIDTH), F32)],
        compiler_params=pltpu.CompilerParams(dimension_semantics=("arbitrary",)),
        name="rglru_scan",
    )(xr, gr, cw, cb, wax, bax, lam, nw)


def _neg_log2_one_minus_beta(zp):
    return jnp.log2(1.0 + jnp.exp2(zp))


def _attn_kernel(q_ref, k_ref, v_ref, kb_ref, o_ref, zraw, zbuf, lbuf, sbuf, tbuf, wbuf, cbuf, acc, qsb, umat,
                 *, tq, tk):
    qi = pl.program_id(2)
    per_tile = tq // tk
    top = per_tile * (qi + 1) - 1
    n = jnp.maximum(top + 1, 4)
    q = q_ref[0]
    lane = _iota((1, PAIR), 1)
    lo_half = lane < HEAD_DIM
    zero = jnp.zeros((), BF16)
    r_i = _iota((tq, tk), 0)
    c_i = _iota((tq, tk), 1)
    qsb[0] = jnp.where(lo_half, q, zero)
    qsb[1] = jnp.where(lo_half, zero, q)
    umat[...] = (_iota((tk, tk), 0) >= _iota((tk, tk), 1)).astype(BF16)

    cbuf[...] = jnp.zeros_like(cbuf)
    acc[...] = jnp.zeros_like(acc)

    def stage_qk(b):
        j = jnp.maximum(top - b, 0)
        ks = k_ref[0, pl.ds(pl.multiple_of(j * tk, tk), tk), :]
        for hh in range(2):
            zraw[hh] = _dot_nt(qsb[hh], ks)

    def stage_a(b, on_diagonal=False, maybe_masked=False):
        j = top - b
        if maybe_masked:
            dummy = j < 0
            j = jnp.maximum(j, 0)
        kbias = kb_ref[0, j]
        if maybe_masked:
            kbias = kbias + jnp.where(dummy, NEG_BIG, 0.0)
        for hh in range(2):
            zp = zraw[hh] + kbias[hh:hh + 1, :]
            if on_diagonal:
                shift = tk * (per_tile - 1 - b)
                zp = zp + jnp.where(c_i + shift < r_i, 0.0, NEG_BIG)
            zp = jnp.minimum(zp, LOGIT2_MAX)
            zbuf[hh] = zp
            lbuf[hh] = _neg_log2_one_minus_beta(zp).astype(BF16)

    def stage_b():
        for hh in range(2):
            incl = _dot(lbuf[hh], umat[...])
            sbuf[hh] = zbuf[hh] - incl
            tbuf[hh] = incl[:, 0:1]

    def stage_c1():
        for hh in range(2):
            c = cbuf[hh]
            wbuf[hh] = jnp.exp2(sbuf[hh] + c).astype(BF16)
            cbuf[hh] = c - tbuf[hh]

    def stage_c2(b):
        j = jnp.maximum(top - b, 0)
        vs = v_ref[0, pl.ds(pl.multiple_of(j * tk, tk), tk), :]
        acc[...] += (_dot(wbuf[0], jnp.where(lo_half, vs, zero))
                     + _dot(wbuf[1], jnp.where(lo_half, zero, vs)))

    def first_a(b):
        stage_a(b, on_diagonal=b < per_tile, maybe_masked=b >= per_tile)

    stage_qk(0)
    first_a(0)
    stage_qk(1)
    stage_b()
    first_a(1)
    stage_qk(2)
    stage_c1()
    stage_b()
    first_a(2)
    stage_qk(3)

    def body(i, carry):
        stage_c2(i - 4)
        stage_c1()
        stage_b()
        stage_a(i - 1)
        stage_qk(i)
        return carry

    lax.fori_loop(4, n, body, 0)
    stage_c2(n - 4)
    stage_c1()
    stage_b()
    stage_a(n - 1, maybe_masked=True)
    stage_c2(n - 3)
    stage_c1()
    stage_b()
    stage_c2(n - 2)
    stage_c1()
    stage_c2(n - 1)
    o_ref[0] = acc[...]


def _attn(qb, kb, vb, kbias, *, tq, tk):
    nb, rows, _ = qb.shape
    assert tq % tk == 0 and tq // tk <= 3 and rows % tq == 0
    nk = rows // tk
    return pl.pallas_call(
        functools.partial(_attn_kernel, tq=tq, tk=tk),
        grid=(nb, SB_WIDTH // PAIR, rows // tq),
        in_specs=[pl.BlockSpec((1, tq, PAIR), lambda b, p, i: (b, i, p)),
                  pl.BlockSpec((1, rows, PAIR), lambda b, p, i: (b, 0, p)),
                  pl.BlockSpec((1, rows, PAIR), lambda b, p, i: (b, 0, p)),
                  pl.BlockSpec((1, nk, SUBLANES, tk), lambda b, p, i: (p, 0, 0, 0))],
        out_specs=pl.BlockSpec((1, tq, PAIR), lambda b, p, i: (b, i, p)),
        out_shape=jax.ShapeDtypeStruct((nb, rows, SB_WIDTH), F32),
        scratch_shapes=[pltpu.VMEM((2, tq, tk), F32), pltpu.VMEM((2, tq, tk), F32), pltpu.VMEM((2, tq, tk), BF16),
                        pltpu.VMEM((2, tq, tk), F32), pltpu.VMEM((2, tq, 1), F32), pltpu.VMEM((2, tq, tk), BF16),
                        pltpu.VMEM((2, tq, 1), F32), pltpu.VMEM((tq, PAIR), F32),
                        pltpu.VMEM((2, tq, PAIR), BF16), pltpu.VMEM((tk, tk), BF16)],
        compiler_params=pltpu.CompilerParams(
            dimension_semantics=("arbitrary", "arbitrary", "arbitrary"), vmem_limit_bytes=VMEM_LIMIT),
        name="sb_attention",
    )(qb, kb, vb, kbias)


def _ffn_kernel(x_ref, yssd_ref, ysb_ref, yrg_ref, sbn_ref, wo_ref, n2_ref, wup_ref, cw_ref, cb_ref, wd_ref,
                out_ref, tail_ref, gbuf, *, tm, p0):
    i = pl.program_id(1)

    @pl.when(i == 0)
    def _():
        gbuf[0:SUBLANES, :] = jnp.zeros((SUBLANES, D_FF), F32)

    ysb = _rms(ysb_ref[0], sbn_ref[...]).astype(BF16)
    mix = jnp.concatenate([yssd_ref[0], ysb, yrg_ref[0]], axis=-1)
    xm = x_ref[0] + _dot(mix, wo_ref[...])
    h2 = _rms(xm, n2_ref[...])
    if p0 > 0:
        rows = i * tm + _iota((tm, 1), 0)
        h2 = jnp.where(rows >= p0, h2, 0.0)
    gu = _dot(h2.astype(BF16), wup_ref[...])
    gbuf[SUBLANES:SUBLANES + tm, :] = gu[:, 0:D_FF]
    conv = cb_ref[...]
    for j in range(3):
        conv = conv + gbuf[6 + j:6 + j + tm, :] * cw_ref[j:j + 1, :]
    last = gbuf[tm:tm + SUBLANES, :]
    tail_ref[0] = last
    gbuf[0:SUBLANES, :] = last
    act = (_silu(conv) * gu[:, D_FF:2 * D_FF]).astype(BF16)
    out_ref[0] = xm + _dot(act, wd_ref[...])


def _ffn(x, yssd, ysb, yrg, sbn, wo, n2, wup, cw, cb, wd, *, tm, p0):
    nb, rows, _ = x.shape

    def tile(width):
        return pl.BlockSpec((1, tm, width), lambda b, i: (b, i, 0))

    def const(shape):
        return pl.BlockSpec(shape, lambda b, i: (0,) * len(shape), pipeline_mode=pl.Buffered(1))

    return pl.pallas_call(
        functools.partial(_ffn_kernel, tm=tm, p0=p0),
        grid=(nb, rows // tm),
        in_specs=[tile(D_MODEL), tile(SSD_WIDTH), tile(SB_WIDTH), tile(RG_WIDTH), const((1, SB_WIDTH)),
                  const((D_MODEL, D_MODEL)), const((1, D_MODEL)), const((D_MODEL, 2 * D_FF)),
                  const((3, D_FF)), const((1, D_FF)), const((D_FF, D_MODEL))],
        out_specs=[tile(D_MODEL), pl.BlockSpec((1, SUBLANES, D_FF), lambda b, i: (b, 0, 0))],
        out_shape=[jax.ShapeDtypeStruct((nb, rows, D_MODEL), F32),
                   jax.ShapeDtypeStruct((nb, SUBLANES, D_FF), F32)],
        scratch_shapes=[pltpu.VMEM((SUBLANES + tm, D_FF), F32)],
        compiler_params=pltpu.CompilerParams(
            dimension_semantics=("arbitrary", "arbitrary"), vmem_limit_bytes=VMEM_LIMIT),
        name="outproj_ffn",
    )(x, yssd, ysb, yrg, sbn, wo, n2, wup, cw, cb, wd)


def _sample_mix_kernel(z_ref, xbc_ref, dt_ref, sbuf_ref, st_ref, xr_ref, gr_ref, rbuf_ref, rh_ref,
                       scw_ref, scb_ref, dtb_ref, alog_ref, dvec_ref, snw_ref,
                       rcw_ref, rcb_ref, wax_ref, bax_ref, lam_ref, rnw_ref,
                       yssd_ref, st_out_ref, sbuf_out_ref, yrg_ref, rh_out_ref, rbuf_out_ref,
                       yscr, *, nseq):
    xbc = xbc_ref[...]
    conv = scb_ref[...] + xbc * scw_ref[3:4, :]
    for j in range(3):
        conv = conv + sbuf_ref[j] * scw_ref[j:j + 1, :]
    sbuf_out_ref[0] = sbuf_ref[1]
    sbuf_out_ref[1] = sbuf_ref[2]
    sbuf_out_ref[2] = xbc
    act = _silu(conv)
    xs = act[:, 0:384]
    bm = act[:, 384:640]
    cm = act[:, 640:896]
    dt = _softplus(dt_ref[...] + dtb_ref[...])
    da = jnp.exp(dt * (-jnp.exp(alog_ref[...])))

    lane = _iota((1, LANES), 1)
    lo_half = lane < HEAD_DIM
    rowp = _iota((LANES, 1), 0)
    top_half = rowp < HEAD_DIM
    pad_rows = LANES - nseq
    yscr[...] = jnp.zeros_like(yscr)
    for pr in range(3):
        ha, hb = 2 * pr, 2 * pr + 1
        ga, gb = ha // 3, hb // 3
        xp = xs[:, pr * LANES:(pr + 1) * LANES]
        xd = xp * jnp.where(lo_half, dt[:, ha:ha + 1], dt[:, hb:hb + 1])
        xd_t = jnp.concatenate([xd, jnp.zeros((pad_rows, LANES), F32)], axis=0).T
        for b in range(nseq):
            hpair = st_ref[b, pr * LANES:(pr + 1) * LANES, :]
            keep = jnp.where(top_half, da[b:b + 1, ha:ha + 1], da[b:b + 1, hb:hb + 1])
            brow = jnp.where(top_half, bm[b:b + 1, ga * LANES:(ga + 1) * LANES],
                             bm[b:b + 1, gb * LANES:(gb + 1) * LANES])
            hnew = hpair * keep + xd_t[:, b:b + 1] * brow
            st_out_ref[b, pr * LANES:(pr + 1) * LANES, :] = hnew
            crow = jnp.where(top_half, cm[b:b + 1, ga * LANES:(ga + 1) * LANES],
                             cm[b:b + 1, gb * LANES:(gb + 1) * LANES])
            ycol = jnp.sum(hnew * crow, axis=1, keepdims=True)
            yscr[pr, :, b:b + 1] = ycol
    ys = []
    for pr in range(3):
        ys.append(yscr[pr].T[0:nseq, :])
    y = jnp.concatenate(ys, axis=-1) + xs * dvec_ref[...]
    y = y * _silu(z_ref[...])
    yssd_ref[...] = _rms(y, snw_ref[...]).astype(BF16)

    xr = xr_ref[...]
    xc = rcb_ref[...] + xr * rcw_ref[3:4, :]
    for j in range(3):
        xc = xc + rbuf_ref[j] * rcw_ref[j:j + 1, :]
    rbuf_out_ref[0] = rbuf_ref[1]
    rbuf_out_ref[1] = rbuf_ref[2]
    rbuf_out_ref[2] = xr
    a, u = _rg_gates(xc, wax_ref, bax_ref, lam_ref)
    h = a * rh_ref[...] + u
    rh_out_ref[...] = h
    yrg_ref[...] = _rms(h * _gelu_tanh(gr_ref[...]), rnw_ref[...]).astype(BF16)


def _sample_mix(z, xbc, dt, sbuf, st, xr, gr, rbuf, rh, scw, scb, dtb, alog, dvec, snw,
                rcw, rcb, wax, bax, lam, rnw):
    nseq = z.shape[0]
    return pl.pallas_call(
        functools.partial(_sample_mix_kernel, nseq=nseq),
        out_shape=[jax.ShapeDtypeStruct((nseq, SSD_WIDTH), BF16),
                   jax.ShapeDtypeStruct((nseq, SSD_WIDTH, SSD_STATE), F32),
                   jax.ShapeDtypeStruct((3, nseq, SSD_XBC), F32),
                   jax.ShapeDtypeStruct((nseq, RG_WIDTH), BF16),
                   jax.ShapeDtypeStruct((nseq, RG_WIDTH), F32),
                   jax.ShapeDtypeStruct((3, nseq, RG_WIDTH), F32)],
        scratch_shapes=[pltpu.VMEM((3, LANES, LANES), F32)],
        compiler_params=pltpu.CompilerParams(vmem_limit_bytes=VMEM_LIMIT),
        name="sample_mixers",
    )(z, xbc, dt, sbuf, st, xr, gr, rbuf, rh, scw, scb, dtb, alog, dvec, snw, rcw, rcb, wax, bax, lam, rnw)


def _sample_attn_kernel(pt_ref, q_ref, bias_ref, *refs, npg):
    del pt_ref
    k_refs = refs[0:npg]
    v_refs = refs[npg:2 * npg]
    o_ref = refs[2 * npg]
    carry_ref, acc_ref = refs[2 * npg + 1:]
    g = pl.program_id(1)

    @pl.when(g == 0)
    def _():
        carry_ref[...] = jnp.zeros_like(carry_ref)
        acc_ref[...] = jnp.zeros_like(acc_ref)

    qb = q_ref[0]
    r_i = _iota((PAGE, PAGE), 0)
    c_i = _iota((PAGE, PAGE), 1)
    incl_mat = (r_i >= c_i).astype(BF16)
    pad = jnp.zeros((SUBLANES - SB_HEADS, PAGE), F32)
    zs, ls = [], []
    for r in range(npg):
        prod = k_refs[r][0] * qb
        rows = [jnp.sum(prod[h * HEAD_DIM:(h + 1) * HEAD_DIM, :], axis=0, keepdims=True)
                for h in range(SB_HEADS)]
        zp = jnp.concatenate(rows + [pad], axis=0) + bias_ref[...]
        zp = jnp.minimum(zp, LOGIT2_MAX)
        zs.append(zp)
        ls.append(_neg_log2_one_minus_beta(zp).astype(BF16))
    carry = carry_ref[...]
    for r in range(npg):
        incl = _dot(ls[r], incl_mat)
        w = jnp.exp2(zs[r] - incl + carry)
        for h in range(SB_HEADS):
            hs = slice(h * HEAD_DIM, (h + 1) * HEAD_DIM)
            acc_ref[hs, :] += jnp.broadcast_to(w[h:h + 1, :], (HEAD_DIM, PAGE)) * v_refs[r][0, hs, :]
        carry = carry - incl[:, 0:1]
    carry_ref[...] = carry

    @pl.when(g == pl.num_programs(1) - 1)
    def _():
        o_ref[0] = jnp.sum(acc_ref[...], axis=1, keepdims=True)


def _sample_attn(pt_flat, q3, bias8, ck, cv, *, nseq, n_pages, npg):
    steps = n_pages // npg

    def page_spec(r):
        def imap(b, g, pt):
            return (pt[b * n_pages + (n_pages - 1 - (g * npg + r))], 0, 0)
        return pl.BlockSpec((1, SB_WIDTH, PAGE), imap)

    grid_spec = pltpu.PrefetchScalarGridSpec(
        num_scalar_prefetch=1,
        grid=(nseq, steps),
        in_specs=[pl.BlockSpec((1, SB_WIDTH, PAGE), lambda b, g, pt: (b, 0, 0)),
                  pl.BlockSpec((SUBLANES, PAGE), lambda b, g, pt: (0, 0))]
                 + [page_spec(r) for r in range(npg)] + [page_spec(r) for r in range(npg)],
        out_specs=pl.BlockSpec((1, SB_WIDTH, 1), lambda b, g, pt: (b, 0, 0)),
        scratch_shapes=[pltpu.VMEM((SUBLANES, 1), F32), pltpu.VMEM((SB_WIDTH, PAGE), F32)],
    )
    return pl.pallas_call(
        functools.partial(_sample_attn_kernel, npg=npg),
        grid_spec=grid_spec,
        out_shape=jax.ShapeDtypeStruct((nseq, SB_WIDTH, 1), F32),
        compiler_params=pltpu.CompilerParams(dimension_semantics=("arbitrary", "arbitrary")),
        name="sample_attention",
    )(pt_flat, q3, bias8, *([ck] * npg), *([cv] * npg))


def _sample_ffn_kernel(x_ref, yssd_ref, ysb_ref, yrg_ref, fbuf_ref, sbn_ref, wo_ref, n2_ref, wup_ref,
                       cw_ref, cb_ref, wd_ref, out_ref, fbuf_out_ref):
    ysb = _rms(ysb_ref[...], sbn_ref[...]).astype(BF16)
    mix = jnp.concatenate([yssd_ref[...], ysb, yrg_ref[...]], axis=-1)
    xm = x_ref[...] + _dot(mix, wo_ref[...])
    h2 = _rms(xm, n2_ref[...])
    gu = _dot(h2.astype(BF16), wup_ref[...])
    g = gu[:, 0:D_FF]
    conv = cb_ref[...] + fbuf_ref[0] * cw_ref[0:1, :] + fbuf_ref[1] * cw_ref[1:2, :] + g * cw_ref[2:3, :]
    fbuf_out_ref[0] = fbuf_ref[1]
    fbuf_out_ref[1] = g
    act = (_silu(conv) * gu[:, D_FF:2 * D_FF]).astype(BF16)
    out_ref[...] = xm + _dot(act, wd_ref[...])


def _sample_ffn(x, yssd, ysb, yrg, fbuf, sbn, wo, n2, wup, cw, cb, wd):
    nseq = x.shape[0]
    return pl.pallas_call(
        _sample_ffn_kernel,
        out_shape=[jax.ShapeDtypeStruct((nseq, D_MODEL), F32),
                   jax.ShapeDtypeStruct((2, nseq, D_FF), F32)],
        compiler_params=pltpu.CompilerParams(vmem_limit_bytes=VMEM_LIMIT),
        name="sample_outproj_ffn",
    )(x, yssd, ysb, yrg, fbuf, sbn, wo, n2, wup, cw, cb, wd)


def _pad_cols(a, width):
    return jnp.pad(a, ((0, 0), (0, width - a.shape[-1])))


def _block_diag(w):
    nblk, bi, bj = w.shape
    out = jnp.zeros((nblk * bi, nblk * bj), w.dtype)
    for h in range(nblk):
        out = out.at[h * bi:(h + 1) * bi, h * bj:(h + 1) * bj].set(w[h])
    return out


def kernel(x_prompt, x_sample, cache_k, cache_v, page_table, state_ssm, state_ssm_conv, state_rg, state_rg_conv, state_ffn_conv, meta_tokens, norm1, w_in, ssd_conv_w, ssd_conv_b, ssd_dt_bias, ssd_a_log, ssd_d, ssd_norm, q_norm, k_norm, sb_bias, sb_out_norm, rg_conv_w, rg_conv_b, rg_wa, rg_ba, rg_wx, rg_bx, rg_lambda, rg_out_norm, w_out, norm2, w_up, ffn_conv_w, ffn_conv_b, w_down):
    depth = w_in.shape[0]
    nb, seq, _ = x_prompt.shape
    nseq = x_sample.shape[0]
    n_pool = cache_k.shape[1]
    n_pages = page_table.shape[1]
    t_real = N_META + seq
    t_pad = -(-t_real // ATT_TQ) * ATT_TQ
    p0 = t_pad - t_real
    nk = t_pad // ATT_TK

    meta = jnp.broadcast_to(meta_tokens[None], (nb, N_META, D_MODEL))
    xp = jnp.concatenate([jnp.zeros((nb, p0, D_MODEL), F32), meta, x_prompt], axis=1)
    xs = x_sample.reshape(1, nseq, D_MODEL)

    head_of = np.arange(SB_WIDTH) // HEAD_DIM
    bd = jnp.asarray(head_of[:, None] == head_of[None, :], BF16)
    key_valid = jnp.arange(t_pad) >= p0
    ck = cache_k.transpose(0, 1, 3, 4, 2).reshape(depth * n_pool, SB_WIDTH, PAGE)
    cv = cache_v.transpose(0, 1, 3, 4, 2).reshape(depth * n_pool, SB_WIDTH, PAGE)

    outs_p, outs_s = [], []
    for l in range(depth):
        wz, wxbc, wdt, wq, wk, wv, wxr, wgr = jnp.split(
            w_in[l].T, np.cumsum([384, 896, 6, 384, 384, 384, 256])[:].tolist(), axis=0)
        wdt = jnp.pad(wdt, ((0, DT_COLS - wdt.shape[0]), (0, 0)))
        w1 = jnp.concatenate([wz, wxbc, wq, wk, wv, wxr, wgr, wdt], axis=0).astype(BF16)
        g1 = norm1[l][None]
        qn = jnp.tile(q_norm[l], SB_HEADS)[None]
        kn = jnp.tile(k_norm[l], SB_HEADS)[None]
        dtb = _pad_cols(ssd_dt_bias[l][None], DT_COLS)
        alog = _pad_cols(ssd_a_log[l][None], DT_COLS)
        dvec = jnp.repeat(ssd_d[l], HEAD_DIM)[None]
        snw = ssd_norm[l][None]
        scw, scb = ssd_conv_w[l], ssd_conv_b[l][None]
        rcw, rcb = rg_conv_w[l], rg_conv_b[l][None]
        wax = jnp.concatenate([_block_diag(rg_wa[l]), _block_diag(rg_wx[l])], axis=1)
        bax = jnp.concatenate([rg_ba[l].reshape(1, RG_WIDTH), rg_bx[l].reshape(1, RG_WIDTH)], axis=1)
        lam = rg_lambda[l][None]
        rnw = rg_out_norm[l][None]
        sbn = sb_out_norm[l][None]
        wo = w_out[l].astype(BF16)
        n2 = norm2[l][None]
        wup = w_up[l].astype(BF16)
        fcw, fcb = ffn_conv_w[l], ffn_conv_b[l][None]
        wd = w_down[l].astype(BF16)

        z, xbc, dt, qb, kb, vb, kf, vf, xr, gr = _inproj(xp, g1, w1, bd, qn, kn, tm=ROW_TILE, p0=p0)
        yssd, ssm_p, sconv_p = _ssd(xbc, z, dt, scw, scb, dtb, alog, dvec, snw, c=CHUNK, p0=p0)
        yrg, rgh_p, rconv_p = _rg(xr, gr, rcw, rcb, wax, bax, lam, rnw, c=CHUNK, p0=p0)
        bias2 = LOG2E * sb_bias[l]
        kbias = jnp.where(key_valid[None, :], bias2[:, None], NEG_BIG)
        kbias = kbias.reshape(3, 2, nk, ATT_TK).transpose(0, 2, 1, 3)
        kbias = jnp.pad(kbias, ((0, 0), (0, 0), (0, SUBLANES - 2), (0, 0)))
        ysb = _attn(qb, kb, vb, kbias, tq=ATT_TQ, tk=ATT_TK)
        xp, fconv_p = _ffn(xp, yssd, ysb, yrg, sbn, wo, n2, wup, fcw, fcb, wd, tm=ROW_TILE, p0=p0)
        outs_p.append((kf[:, p0:].reshape(nb, t_real, SB_HEADS, HEAD_DIM),
                       vf[:, p0:].reshape(nb, t_real, SB_HEADS, HEAD_DIM),
                       ssm_p.reshape(nb, SSD_HEADS, HEAD_DIM, SSD_STATE),
                       sconv_p[:, SUBLANES - 3:],
                       rgh_p.reshape(nb, RG_WIDTH),
                       rconv_p[:, SUBLANES - 3:],
                       fconv_p[:, SUBLANES - 2:]))

        z, xbc, dt, qb, kb, vb, kf, vf, xr, gr = _inproj(xs, g1, w1, bd, qn, kn, tm=nseq, p0=0)
        yssd, ssm_s, sconv_s, yrg, rgh_s, rconv_s = _sample_mix(
            z[0], xbc[0], dt[0], state_ssm_conv[l].transpose(1, 0, 2),
            state_ssm[l].reshape(nseq, SSD_WIDTH, SSD_STATE), xr[0], gr[0],
            state_rg_conv[l].transpose(1, 0, 2), state_rg[l],
            scw, scb, dtb, alog, dvec, snw, rcw, rcb, wax, bax, lam, rnw)
        bias8 = jnp.broadcast_to(jnp.pad(bias2, (0, SUBLANES - SB_HEADS))[:, None], (SUBLANES, PAGE))
        pt_flat = (page_table + l * n_pool).reshape(-1).astype(jnp.int32)
        q_lanes = jnp.broadcast_to(qb[0].astype(F32)[:, :, None], (nseq, SB_WIDTH, PAGE))
        ysb = _sample_attn(pt_flat, q_lanes, bias8, ck, cv,
                           nseq=nseq, n_pages=n_pages, npg=PAGES_PER_STEP)
        xs2, fconv_s = _sample_ffn(xs[0], yssd, ysb.reshape(nseq, SB_WIDTH), yrg,
                                   state_ffn_conv[l].transpose(1, 0, 2), sbn, wo, n2, wup, fcw, fcb, wd)
        xs = xs2[None]
        outs_s.append((kf.reshape(nseq, 1, SB_HEADS, HEAD_DIM),
                       vf.reshape(nseq, 1, SB_HEADS, HEAD_DIM),
                       ssm_s.reshape(nseq, SSD_HEADS, HEAD_DIM, SSD_STATE),
                       sconv_s.transpose(1, 0, 2),
                       rgh_s,
                       rconv_s.transpose(1, 0, 2),
                       fconv_s.transpose(1, 0, 2)))

    def stk(outs, i):
        return jnp.stack([o[i] for o in outs], axis=0)

    y_prompt = xp[:, p0 + N_META:]
    y_sample = xs.reshape(nseq, 1, D_MODEL)
    return (y_prompt, y_sample,
            stk(outs_p, 0), stk(outs_p, 1), stk(outs_p, 2), stk(outs_p, 3), stk(outs_p, 4), stk(outs_p, 5), stk(outs_p, 6),
            stk(outs_s, 0), stk(outs_s, 1), stk(outs_s, 2), stk(outs_s, 3), stk(outs_s, 4), stk(outs_s, 5), stk(outs_s, 6))
```

```python
import functools

import jax
import jax.numpy as jnp
import numpy as np
from jax import lax
from jax.experimental import pallas as pl
from jax.experimental.pallas import tpu as pltpu

F32 = jnp.float32
BF16 = jnp.bfloat16

D_MODEL = 1024
N_META = 16
HEAD_DIM = 64
SSD_WIDTH = 384
SSD_HEADS = 6
SSD_STATE = 128
SSD_XBC = 896
SB_WIDTH = 384
SB_HEADS = 6
RG_WIDTH = 256
RG_C = 8.0
D_FF = 2816
EPS = 1e-6

LANES = 128
SUBLANES = 8
PAIR = 2 * HEAD_DIM
DT_COLS = LANES
IN_COLS_PAD = 2 * SSD_WIDTH + 512 + 3 * SB_WIDTH + 2 * RG_WIDTH + DT_COLS
NEG_BIG = -1e30
LOG2E = 1.4426950408889634
LOGIT2_MAX = 126.0

ROW_TILE = 256
CHUNK = 128
ATT_TQ = 256
ATT_TK = 256
PAGE = 128
PAGES_PER_STEP = 16
VMEM_LIMIT = 56 * 1024 * 1024

NT_DIMS = (((1,), (1,)), ((), ()))


def _dot(a, b):
    return jnp.dot(a, b, preferred_element_type=F32)


def _dot_nt(a, b):
    return lax.dot_general(a, b, NT_DIMS, preferred_element_type=F32)


def _split_bf16(x, n):
    parts = []
    r = x
    for _ in range(n):
        p = r.astype(BF16)
        parts.append(p)
        r = r - p.astype(F32)
    return parts


def _dot_split_lhs(x, m, n):
    out = None
    for p in _split_bf16(x, n):
        t = _dot(p, m)
        out = t if out is None else out + t
    return out


def _dot_split_rhs(m, x, n):
    out = None
    for p in _split_bf16(x, n):
        t = _dot(m, p)
        out = t if out is None else out + t
    return out


def _rms(x, g):
    return x * lax.rsqrt(jnp.mean(x * x, axis=-1, keepdims=True) + EPS) * g


def _softplus(x):
    return jnp.maximum(x, 0.0) + jnp.log(1.0 + jnp.exp(-jnp.abs(x)))


def _silu(x):
    return x * jax.nn.sigmoid(x)


def _gelu_tanh(x):
    return 0.5 * x * (1.0 + jnp.tanh(0.7978845608028654 * (x + 0.044715 * (x * x * x))))


def _iota(shape, dim):
    return lax.broadcasted_iota(jnp.int32, shape, dim)


def _inproj_kernel(x_ref, g_ref, w_ref, bd_ref, qn_ref, kn_ref,
                   z_ref, xbc_ref, dt_ref, qb_ref, kb_ref, vb_ref, k_ref, v_ref, xr_ref, gr_ref,
                   *, tm, p0):
    x = x_ref[0]
    h = _rms(x, g_ref[...])
    if p0 > 0:
        rows = pl.program_id(1) * tm + _iota((tm, 1), 0)
        h = jnp.where(rows >= p0, h, 0.0)
    proj = _dot_nt(h.astype(BF16), w_ref[...])
    z_ref[0] = proj[:, 0:384]
    xbc_ref[0] = proj[:, 384:1280]
    q = proj[:, 1280:1664]
    k = proj[:, 1664:2048]
    v = proj[:, 2048:2432]
    xr_ref[0] = proj[:, 2432:2688]
    gr_ref[0] = proj[:, 2688:2944]
    dt_ref[0] = proj[:, 2944:3072]
    bd = bd_ref[...]
    q_ms = _dot_split_lhs(q * q, bd, 2) * (1.0 / HEAD_DIM)
    k_ms = _dot_split_lhs(k * k, bd, 2) * (1.0 / HEAD_DIM)
    qn = q * lax.rsqrt(q_ms + EPS) * qn_ref[...]
    kn = k * lax.rsqrt(k_ms + EPS) * kn_ref[...]
    qb_ref[0] = (qn * (LOG2E * HEAD_DIM ** -0.5)).astype(BF16)
    kb_ref[0] = kn.astype(BF16)
    vb_ref[0] = v.astype(BF16)
    k_ref[0] = kn
    v_ref[0] = v


def _inproj(x, g, w, bd, qn, kn, *, tm, p0):
    nb, rows, _ = x.shape
    grid = (nb, rows // tm)

    def tile(width):
        return pl.BlockSpec((1, tm, width), lambda b, i: (b, i, 0))

    def const(shape):
        return pl.BlockSpec(shape, lambda b, i: (0,) * len(shape), pipeline_mode=pl.Buffered(1))

    widths = (384, 896, DT_COLS, 384, 384, 384, 384, 384, 256, 256)
    dtypes = (F32, F32, F32, BF16, BF16, BF16, F32, F32, F32, F32)
    return pl.pallas_call(
        functools.partial(_inproj_kernel, tm=tm, p0=p0),
        grid=grid,
        in_specs=[tile(D_MODEL), const((1, D_MODEL)), const((IN_COLS_PAD, D_MODEL)),
                  const((384, 384)), const((1, 384)), const((1, 384))],
        out_specs=[tile(wd) for wd in widths],
        out_shape=[jax.ShapeDtypeStruct((nb, rows, wd), dt) for wd, dt in zip(widths, dtypes)],
        compiler_params=pltpu.CompilerParams(
            dimension_semantics=("parallel", "parallel"), vmem_limit_bytes=VMEM_LIMIT),
        name="inproj",
    )(x, g, w, bd, qn, kn)


def _ssd_kernel(xbc_ref, z_ref, dt_ref, cw_ref, cb_ref, dtb_ref, alog_ref, dvec_ref, nw_ref,
                y_ref, st_ref, tail_ref, xbuf, hst, *, nb, c, p0):
    ci = pl.program_id(0)

    @pl.when(ci == 0)
    def _():
        xbuf[:, 0:SUBLANES, :] = jnp.zeros((nb, SUBLANES, SSD_XBC), F32)
        hst[...] = jnp.zeros_like(hst)

    for bi in range(nb):
        _ssd_chunk(bi, ci, xbc_ref, z_ref, dt_ref, cw_ref, cb_ref, dtb_ref, alog_ref, dvec_ref, nw_ref,
                   y_ref, tail_ref, xbuf, hst, c=c, p0=p0)

    @pl.when(ci == pl.num_programs(0) - 1)
    def _():
        st_ref[...] = hst[...]


def _ssd_chunk(bi, ci, xbc_ref, z_ref, dt_ref, cw_ref, cb_ref, dtb_ref, alog_ref, dvec_ref, nw_ref,
               y_ref, tail_ref, xbuf, hst, *, c, p0):
    xbuf[bi, SUBLANES:SUBLANES + c, :] = xbc_ref[bi]
    conv = cb_ref[...]
    for j in range(4):
        conv = conv + xbuf[bi, 5 + j:5 + j + c, :] * cw_ref[j:j + 1, :]
    last = xbuf[bi, c:c + SUBLANES, :]
    tail_ref[bi] = last
    xbuf[bi, 0:SUBLANES, :] = last
    act = _silu(conv)
    xs = act[:, 0:384]
    bm = act[:, 384:640]
    cm = act[:, 640:896]

    rows = ci * c + _iota((c, 1), 0)
    dt = _softplus(dt_ref[bi] + dtb_ref[...])
    dt = jnp.where(rows >= p0, dt, 0.0)
    a = dt * (-jnp.exp(alog_ref[...]))
    r_i = _iota((c, c), 0)
    c_i = _iota((c, c), 1)
    tri = r_i >= c_i
    cum = _dot_split_rhs(tri.astype(BF16), a, 3)
    cum_t = cum.T
    cum_last = cum[c - 1:c, :]
    e_cum = jnp.exp(cum)
    dec = jnp.exp(cum_last - cum) * dt
    e_last = jnp.exp(cum_last)

    lane = _iota((1, LANES), 1)
    lo_half = lane < HEAD_DIM
    rowp = _iota((LANES, 1), 0)
    top_half = rowp < HEAD_DIM
    cbs = [_dot_nt(cm[:, g * LANES:(g + 1) * LANES], bm[:, g * LANES:(g + 1) * LANES]) for g in range(2)]

    def col(m, h):
        return m[:, h:h + 1]

    ys = []
    for pr in range(3):
        ha, hb = 2 * pr, 2 * pr + 1
        ga, gb = ha // 3, hb // 3
        xp = xs[:, pr * LANES:(pr + 1) * LANES]
        xd = xp * jnp.where(lo_half, col(dt, ha), col(dt, hb))
        hpair = hst[bi, pr * LANES:(pr + 1) * LANES, :]
        y_intra = []
        for h, g in ((ha, ga), (hb, gb)):
            seg = col(cum, h) - cum_t[h:h + 1, :]
            lm = jnp.exp(jnp.where(tri, seg, -jnp.inf))
            y_intra.append(_dot(cbs[g] * lm, xd))
        yi_a = _dot_nt(cm[:, ga * LANES:(ga + 1) * LANES], hpair)
        yi_b = yi_a if gb == ga else _dot_nt(cm[:, gb * LANES:(gb + 1) * LANES], hpair)
        y = jnp.where(lo_half, y_intra[0] + yi_a * col(e_cum, ha), y_intra[1] + yi_b * col(e_cum, hb))
        ys.append(y + xp * dvec_ref[:, pr * LANES:(pr + 1) * LANES])
        xe_t = (xp * jnp.where(lo_half, col(dec, ha), col(dec, hb))).T
        hn_a = _dot(xe_t, bm[:, ga * LANES:(ga + 1) * LANES])
        hn_b = hn_a if gb == ga else _dot(xe_t, bm[:, gb * LANES:(gb + 1) * LANES])
        keep = jnp.where(top_half, col(e_last, ha), col(e_last, hb))
        hst[bi, pr * LANES:(pr + 1) * LANES, :] = hpair * keep + jnp.where(top_half, hn_a, hn_b)

    y = jnp.concatenate(ys, axis=-1) * _silu(z_ref[bi])
    y_ref[bi] = _rms(y, nw_ref[...]).astype(BF16)


def _ssd(xbc, z, dt, cw, cb, dtb, alog, dvec, nw, *, c, p0):
    nb, rows, _ = xbc.shape

    def tile(width):
        return pl.BlockSpec((nb, c, width), lambda i: (0, i, 0))

    def const(shape):
        return pl.BlockSpec(shape, lambda i: (0,) * len(shape))

    return pl.pallas_call(
        functools.partial(_ssd_kernel, nb=nb, c=c, p0=p0),
        grid=(rows // c,),
        in_specs=[tile(SSD_XBC), tile(SSD_WIDTH), tile(DT_COLS), const((4, SSD_XBC)), const((1, SSD_XBC)),
                  const((1, DT_COLS)), const((1, DT_COLS)), const((1, SSD_WIDTH)), const((1, SSD_WIDTH))],
        out_specs=[tile(SSD_WIDTH), const((nb, SSD_WIDTH, SSD_STATE)), const((nb, SUBLANES, SSD_XBC))],
        out_shape=[jax.ShapeDtypeStruct((nb, rows, SSD_WIDTH), BF16),
                   jax.ShapeDtypeStruct((nb, SSD_WIDTH, SSD_STATE), F32),
                   jax.ShapeDtypeStruct((nb, SUBLANES, SSD_XBC), F32)],
        scratch_shapes=[pltpu.VMEM((nb, SUBLANES + c, SSD_XBC), F32),
                        pltpu.VMEM((nb, SSD_WIDTH, SSD_STATE), F32)],
        compiler_params=pltpu.CompilerParams(
            dimension_semantics=("arbitrary",), vmem_limit_bytes=VMEM_LIMIT),
        name="ssd_scan",
    )(xbc, z, dt, cw, cb, dtb, alog, dvec, nw)


def _rg_gates(xc, wax_ref, bax_ref, lam_ref):
    ri = jax.nn.sigmoid(_dot(xc, wax_ref[...]) + bax_ref[...])
    r = ri[:, 0:RG_WIDTH]
    i = ri[:, RG_WIDTH:2 * RG_WIDTH]
    log_a = RG_C * r * (-_softplus(-lam_ref[...]))
    a = jnp.exp(log_a)
    u = jnp.sqrt(1.0 - jnp.exp(2.0 * log_a)) * (i * xc)
    return a, u


def _rg_kernel(xr_ref, gr_ref, cw_ref, cb_ref, wax_ref, bax_ref, lam_ref, nw_ref,
               y_ref, h_ref, tail_ref, xbuf, hprev, *, nb, c, p0):
    ci = pl.program_id(0)

    @pl.when(ci == 0)
    def _():
        xbuf[:, 0:SUBLANES, :] = jnp.zeros((nb, SUBLANES, RG_WIDTH), F32)
        hprev[...] = jnp.zeros_like(hprev)

    for bi in range(nb):
        _rg_chunk(bi, ci, xr_ref, gr_ref, cw_ref, cb_ref, wax_ref, bax_ref, lam_ref, nw_ref,
                  y_ref, h_ref, tail_ref, xbuf, hprev, c=c, p0=p0)


def _rg_chunk(bi, ci, xr_ref, gr_ref, cw_ref, cb_ref, wax_ref, bax_ref, lam_ref, nw_ref,
              y_ref, h_ref, tail_ref, xbuf, hprev, *, c, p0):
    xbuf[bi, SUBLANES:SUBLANES + c, :] = xr_ref[bi]
    xc = cb_ref[...]
    for j in range(4):
        xc = xc + xbuf[bi, 5 + j:5 + j + c, :] * cw_ref[j:j + 1, :]
    last = xbuf[bi, c:c + SUBLANES, :]
    tail_ref[bi] = last
    xbuf[bi, 0:SUBLANES, :] = last

    a, u = _rg_gates(xc, wax_ref, bax_ref, lam_ref)
    rowi = _iota((c, 1), 0)
    valid = (ci * c + rowi) >= p0
    a = jnp.where(valid, a, 1.0)
    u = jnp.where(valid, u, 0.0)
    d = 1
    while d < c:
        a_sh = jnp.where(rowi >= d, pltpu.roll(a, d, 0), 1.0)
        u_sh = jnp.where(rowi >= d, pltpu.roll(u, d, 0), 0.0)
        u = a * u_sh + u
        a = a * a_sh
        d *= 2
    h = a * hprev[bi] + u
    h_last = h[c - 1:c, :]
    hprev[bi] = h_last
    h_ref[bi] = h_last
    y = h * _gelu_tanh(gr_ref[bi])
    y_ref[bi] = _rms(y, nw_ref[...]).astype(BF16)


def _rg(xr, gr, cw, cb, wax, bax, lam, nw, *, c, p0):
    nb, rows, _ = xr.shape

    def tile():
        return pl.BlockSpec((nb, c, RG_WIDTH), lambda i: (0, i, 0))

    def const(shape):
        return pl.BlockSpec(shape, lambda i: (0,) * len(shape))

    return pl.pallas_call(
        functools.partial(_rg_kernel, nb=nb, c=c, p0=p0),
        grid=(rows // c,),
        in_specs=[tile(), tile(), const((4, RG_WIDTH)), const((1, RG_WIDTH)), const((RG_WIDTH, 2 * RG_WIDTH)),
                  const((1, 2 * RG_WIDTH)), const((1, RG_WIDTH)), const((1, RG_WIDTH))],
        out_specs=[tile(), const((nb, 1, RG_WIDTH)), const((nb, SUBLANES, RG_WIDTH))],
        out_shape=[jax.ShapeDtypeStruct((nb, rows, RG_WIDTH), BF16),
                   jax.ShapeDtypeStruct((nb, 1, RG_WIDTH), F32),
                   jax.ShapeDtypeStruct((nb, SUBLANES, RG_WIDTH), F32)],
        scratch_shapes=[pltpu.VMEM((nb, SUBLANES + c, RG_WIDTH), F32), pltpu.VMEM((nb, 1, RG_WIDTH), F32)],
        compiler_params=pltpu.CompilerParams(dimension_semantics=("arbitrary",)),
        name="rglru_scan",
    )(xr, gr, cw, cb, wax, bax, lam, nw)


def _neg_log2_one_minus_beta(zp):
    return jnp.log2(1.0 + jnp.exp2(zp))


def _attn_kernel(q_ref, k_ref, v_ref, kb_ref, o_ref, zraw, zbuf, lbuf, sbuf, tbuf, wbuf, cbuf, acc, qsb, umat,
                 *, tq, tk):
    qi = pl.program_id(2)
    per_tile = tq // tk
    top = per_tile * (qi + 1) - 1
    n = jnp.maximum(top + 1, 4)
    q = q_ref[0]
    lane = _iota((1, PAIR), 1)
    lo_half = lane < HEAD_DIM
    zero = jnp.zeros((), BF16)
    r_i = _iota((tq, tk), 0)
    c_i = _iota((tq, tk), 1)
    qsb[0] = jnp.where(lo_half, q, zero)
    qsb[1] = jnp.where(lo_half, zero, q)
    umat[...] = (_iota((tk, tk), 0) >= _iota((tk, tk), 1)).astype(BF16)

    cbuf[...] = jnp.zeros_like(cbuf)
    acc[...] = jnp.zeros_like(acc)

    def stage_qk(b):
        j = jnp.maximum(top - b, 0)
        ks = k_ref[0, pl.ds(pl.multiple_of(j * tk, tk), tk), :]
        for hh in range(2):
            zraw[hh] = _dot_nt(qsb[hh], ks)

    def stage_a(b, on_diagonal=False, maybe_masked=False):
        j = top - b
        if maybe_masked:
            dummy = j < 0
            j = jnp.maximum(j, 0)
        kbias = kb_ref[0, j]
        if maybe_masked:
            kbias = kbias + jnp.where(dummy, NEG_BIG, 0.0)
        for hh in range(2):
            zp = zraw[hh] + kbias[hh:hh + 1, :]
            if on_diagonal:
                shift = tk * (per_tile - 1 - b)
                zp = zp + jnp.where(c_i + shift < r_i, 0.0, NEG_BIG)
            zp = jnp.minimum(zp, LOGIT2_MAX)
            zbuf[hh] = zp
            lbuf[hh] = _neg_log2_one_minus_beta(zp).astype(BF16)

    def stage_b():
        for hh in range(2):
            incl = _dot(lbuf[hh], umat[...])
            sbuf[hh] = zbuf[hh] - incl
            tbuf[hh] = incl[:, 0:1]

    def stage_c1():
        for hh in range(2):
            c = cbuf[hh]
            wbuf[hh] = jnp.exp2(sbuf[hh] + c).astype(BF16)
            cbuf[hh] = c - tbuf[hh]

    def stage_c2(b):
        j = jnp.maximum(top - b, 0)
        vs = v_ref[0, pl.ds(pl.multiple_of(j * tk, tk), tk), :]
        acc[...] += (_dot(wbuf[0], jnp.where(lo_half, vs, zero))
                     + _dot(wbuf[1], jnp.where(lo_half, zero, vs)))

    def first_a(b):
        stage_a(b, on_diagonal=b < per_tile, maybe_masked=b >= per_tile)

    stage_qk(0)
    first_a(0)
    stage_qk(1)
    stage_b()
    first_a(1)
    stage_qk(2)
    stage_c1()
    stage_b()
    first_a(2)
    stage_qk(3)

    def body(i, carry):
        stage_c2(i - 4)
        stage_c1()
        stage_b()
        stage_a(i - 1)
        stage_qk(i)
        return carry

    lax.fori_loop(4, n, body, 0)
    stage_c2(n - 4)
    stage_c1()
    stage_b()
    stage_a(n - 1, maybe_masked=True)
    stage_c2(n - 3)
    stage_c1()
    stage_b()
    stage_c2(n - 2)
    stage_c1()
    stage_c2(n - 1)
    o_ref[0] = acc[...]


def _attn(qb, kb, vb, kbias, *, tq, tk):
    nb, rows, _ = qb.shape
    assert tq % tk == 0 and tq // tk <= 3 and rows % tq == 0
    nk = rows // tk
    return pl.pallas_call(
        functools.partial(_attn_kernel, tq=tq, tk=tk),
        grid=(nb, SB_WIDTH // PAIR, rows // tq),
        in_specs=[pl.BlockSpec((1, tq, PAIR), lambda b, p, i: (b, i, p)),
                  pl.BlockSpec((1, rows, PAIR), lambda b, p, i: (b, 0, p)),
                  pl.BlockSpec((1, rows, PAIR), lambda b, p, i: (b, 0, p)),
                  pl.BlockSpec((1, nk, SUBLANES, tk), lambda b, p, i: (p, 0, 0, 0))],
        out_specs=pl.BlockSpec((1, tq, PAIR), lambda b, p, i: (b, i, p)),
        out_shape=jax.ShapeDtypeStruct((nb, rows, SB_WIDTH), F32),
        scratch_shapes=[pltpu.VMEM((2, tq, tk), F32), pltpu.VMEM((2, tq, tk), F32), pltpu.VMEM((2, tq, tk), BF16),
                        pltpu.VMEM((2, tq, tk), F32), pltpu.VMEM((2, tq, 1), F32), pltpu.VMEM((2, tq, tk), BF16),
                        pltpu.VMEM((2, tq, 1), F32), pltpu.VMEM((tq, PAIR), F32),
                        pltpu.VMEM((2, tq, PAIR), BF16), pltpu.VMEM((tk, tk), BF16)],
        compiler_params=pltpu.CompilerParams(
            dimension_semantics=("arbitrary", "arbitrary", "arbitrary"), vmem_limit_bytes=VMEM_LIMIT),
        name="sb_attention",
    )(qb, kb, vb, kbias)


def _ffn_kernel(x_ref, yssd_ref, ysb_ref, yrg_ref, sbn_ref, wo_ref, n2_ref, wup_ref, cw_ref, cb_ref, wd_ref,
                out_ref, tail_ref, gbuf, *, tm, p0):
    i = pl.program_id(1)

    @pl.when(i == 0)
    def _():
        gbuf[0:SUBLANES, :] = jnp.zeros((SUBLANES, D_FF), F32)

    ysb = _rms(ysb_ref[0], sbn_ref[...]).astype(BF16)
    mix = jnp.concatenate([yssd_ref[0], ysb, yrg_ref[0]], axis=-1)
    xm = x_ref[0] + _dot(mix, wo_ref[...])
    h2 = _rms(xm, n2_ref[...])
    if p0 > 0:
        rows = i * tm + _iota((tm, 1), 0)
        h2 = jnp.where(rows >= p0, h2, 0.0)
    gu = _dot(h2.astype(BF16), wup_ref[...])
    gbuf[SUBLANES:SUBLANES + tm, :] = gu[:, 0:D_FF]
    conv = cb_ref[...]
    for j in range(3):
        conv = conv + gbuf[6 + j:6 + j + tm, :] * cw_ref[j:j + 1, :]
    last = gbuf[tm:tm + SUBLANES, :]
    tail_ref[0] = last
    gbuf[0:SUBLANES, :] = last
    act = (_silu(conv) * gu[:, D_FF:2 * D_FF]).astype(BF16)
    out_ref[0] = xm + _dot(act, wd_ref[...])


def _ffn(x, yssd, ysb, yrg, sbn, wo, n2, wup, cw, cb, wd, *, tm, p0, skip_rows):
    nb, rows, _ = x.shape
    assert skip_rows % tm == 0
    skip = skip_rows // tm

    def tile(width):
        return pl.BlockSpec((1, tm, width), lambda b, i: (b, i, 0))

    def const(shape):
        return pl.BlockSpec(shape, lambda b, i: (0,) * len(shape), pipeline_mode=pl.Buffered(1))

    out_tile = pl.BlockSpec((1, tm, D_MODEL), lambda b, i: (b, jnp.maximum(i - skip, 0), 0))

    return pl.pallas_call(
        functools.partial(_ffn_kernel, tm=tm, p0=p0),
        grid=(nb, rows // tm),
        in_specs=[tile(D_MODEL), tile(SSD_WIDTH), tile(SB_WIDTH), tile(RG_WIDTH), const((1, SB_WIDTH)),
                  const((D_MODEL, D_MODEL)), const((1, D_MODEL)), const((D_MODEL, 2 * D_FF)),
                  const((3, D_FF)), const((1, D_FF)), const((D_FF, D_MODEL))],
        out_specs=[out_tile, pl.BlockSpec((1, SUBLANES, D_FF), lambda b, i: (b, 0, 0))],
        out_shape=[jax.ShapeDtypeStruct((nb, rows - skip_rows, D_MODEL), F32),
                   jax.ShapeDtypeStruct((nb, SUBLANES, D_FF), F32)],
        scratch_shapes=[pltpu.VMEM((SUBLANES + tm, D_FF), F32)],
        compiler_params=pltpu.CompilerParams(
            dimension_semantics=("arbitrary", "arbitrary"), vmem_limit_bytes=VMEM_LIMIT),
        name="outproj_ffn",
    )(x, yssd, ysb, yrg, sbn, wo, n2, wup, cw, cb, wd)


def _sample_mix_kernel(z_ref, xbc_ref, dt_ref, sbuf_ref, st_ref, xr_ref, gr_ref, rbuf_ref, rh_ref,
                       scw_ref, scb_ref, dtb_ref, alog_ref, dvec_ref, snw_ref,
                       rcw_ref, rcb_ref, wax_ref, bax_ref, lam_ref, rnw_ref,
                       yssd_ref, st_out_ref, sbuf_out_ref, yrg_ref, rh_out_ref, rbuf_out_ref,
                       yscr, *, nseq):
    xbc = xbc_ref[...]
    conv = scb_ref[...] + xbc * scw_ref[3:4, :]
    for j in range(3):
        conv = conv + sbuf_ref[j] * scw_ref[j:j + 1, :]
    sbuf_out_ref[0] = sbuf_ref[1]
    sbuf_out_ref[1] = sbuf_ref[2]
    sbuf_out_ref[2] = xbc
    act = _silu(conv)
    xs = act[:, 0:384]
    bm = act[:, 384:640]
    cm = act[:, 640:896]
    dt = _softplus(dt_ref[...] + dtb_ref[...])
    da = jnp.exp(dt * (-jnp.exp(alog_ref[...])))

    lane = _iota((1, LANES), 1)
    lo_half = lane < HEAD_DIM
    rowp = _iota((LANES, 1), 0)
    top_half = rowp < HEAD_DIM
    pad_rows = LANES - nseq
    yscr[...] = jnp.zeros_like(yscr)
    for pr in range(3):
        ha, hb = 2 * pr, 2 * pr + 1
        ga, gb = ha // 3, hb // 3
        xp = xs[:, pr * LANES:(pr + 1) * LANES]
        xd = xp * jnp.where(lo_half, dt[:, ha:ha + 1], dt[:, hb:hb + 1])
        xd_t = jnp.concatenate([xd, jnp.zeros((pad_rows, LANES), F32)], axis=0).T
        for b in range(nseq):
            hpair = st_ref[b, pr * LANES:(pr + 1) * LANES, :]
            keep = jnp.where(top_half, da[b:b + 1, ha:ha + 1], da[b:b + 1, hb:hb + 1])
            brow = jnp.where(top_half, bm[b:b + 1, ga * LANES:(ga + 1) * LANES],
                             bm[b:b + 1, gb * LANES:(gb + 1) * LANES])
            hnew = hpair * keep + xd_t[:, b:b + 1] * brow
            st_out_ref[b, pr * LANES:(pr + 1) * LANES, :] = hnew
            crow = jnp.where(top_half, cm[b:b + 1, ga * LANES:(ga + 1) * LANES],
                             cm[b:b + 1, gb * LANES:(gb + 1) * LANES])
            ycol = jnp.sum(hnew * crow, axis=1, keepdims=True)
            yscr[pr, :, b:b + 1] = ycol
    ys = []
    for pr in range(3):
        ys.append(yscr[pr].T[0:nseq, :])
    y = jnp.concatenate(ys, axis=-1) + xs * dvec_ref[...]
    y = y * _silu(z_ref[...])
    yssd_ref[...] = _rms(y, snw_ref[...]).astype(BF16)

    xr = xr_ref[...]
    xc = rcb_ref[...] + xr * rcw_ref[3:4, :]
    for j in range(3):
        xc = xc + rbuf_ref[j] * rcw_ref[j:j + 1, :]
    rbuf_out_ref[0] = rbuf_ref[1]
    rbuf_out_ref[1] = rbuf_ref[2]
    rbuf_out_ref[2] = xr
    a, u = _rg_gates(xc, wax_ref, bax_ref, lam_ref)
    h = a * rh_ref[...] + u
    rh_out_ref[...] = h
    yrg_ref[...] = _rms(h * _gelu_tanh(gr_ref[...]), rnw_ref[...]).astype(BF16)


def _sample_mix(z, xbc, dt, sbuf, st, xr, gr, rbuf, rh, scw, scb, dtb, alog, dvec, snw,
                rcw, rcb, wax, bax, lam, rnw):
    nseq = z.shape[0]
    return pl.pallas_call(
        functools.partial(_sample_mix_kernel, nseq=nseq),
        out_shape=[jax.ShapeDtypeStruct((nseq, SSD_WIDTH), BF16),
                   jax.ShapeDtypeStruct((nseq, SSD_WIDTH, SSD_STATE), F32),
                   jax.ShapeDtypeStruct((3, nseq, SSD_XBC), F32),
                   jax.ShapeDtypeStruct((nseq, RG_WIDTH), BF16),
                   jax.ShapeDtypeStruct((nseq, RG_WIDTH), F32),
                   jax.ShapeDtypeStruct((3, nseq, RG_WIDTH), F32)],
        scratch_shapes=[pltpu.VMEM((3, LANES, LANES), F32)],
        compiler_params=pltpu.CompilerParams(vmem_limit_bytes=VMEM_LIMIT),
        name="sample_mixers",
    )(z, xbc, dt, sbuf, st, xr, gr, rbuf, rh, scw, scb, dtb, alog, dvec, snw, rcw, rcb, wax, bax, lam, rnw)


def _sample_attn_kernel(pt_ref, q_ref, bias_ref, *refs, npg):
    del pt_ref
    k_refs = refs[0:npg]
    v_refs = refs[npg:2 * npg]
    o_ref = refs[2 * npg]
    carry_ref, acc_ref = refs[2 * npg + 1:]
    g = pl.program_id(1)

    @pl.when(g == 0)
    def _():
        carry_ref[...] = jnp.zeros_like(carry_ref)
        acc_ref[...] = jnp.zeros_like(acc_ref)

    qb = q_ref[0]
    r_i = _iota((PAGE, PAGE), 0)
    c_i = _iota((PAGE, PAGE), 1)
    incl_mat = (r_i >= c_i).astype(BF16)
    pad = jnp.zeros((SUBLANES - SB_HEADS, PAGE), F32)
    zs, ls = [], []
    for r in range(npg):
        prod = k_refs[r][0] * qb
        rows = [jnp.sum(prod[h * HEAD_DIM:(h + 1) * HEAD_DIM, :], axis=0, keepdims=True)
                for h in range(SB_HEADS)]
        zp = jnp.concatenate(rows + [pad], axis=0) + bias_ref[...]
        zp = jnp.minimum(zp, LOGIT2_MAX)
        zs.append(zp)
        ls.append(_neg_log2_one_minus_beta(zp).astype(BF16))
    carry = carry_ref[...]
    for r in range(npg):
        incl = _dot(ls[r], incl_mat)
        w = jnp.exp2(zs[r] - incl + carry)
        for h in range(SB_HEADS):
            hs = slice(h * HEAD_DIM, (h + 1) * HEAD_DIM)
            acc_ref[hs, :] += jnp.broadcast_to(w[h:h + 1, :], (HEAD_DIM, PAGE)) * v_refs[r][0, hs, :]
        carry = carry - incl[:, 0:1]
    carry_ref[...] = carry

    @pl.when(g == pl.num_programs(1) - 1)
    def _():
        o_ref[0] = jnp.sum(acc_ref[...], axis=1, keepdims=True)


def _sample_attn(pt_flat, q3, bias8, ck, cv, *, nseq, n_pages, npg):
    steps = n_pages // npg

    def page_spec(r):
        def imap(b, g, pt):
            return (pt[b * n_pages + (n_pages - 1 - (g * npg + r))], 0, 0)
        return pl.BlockSpec((1, SB_WIDTH, PAGE), imap)

    grid_spec = pltpu.PrefetchScalarGridSpec(
        num_scalar_prefetch=1,
        grid=(nseq, steps),
        in_specs=[pl.BlockSpec((1, SB_WIDTH, PAGE), lambda b, g, pt: (b, 0, 0)),
                  pl.BlockSpec((SUBLANES, PAGE), lambda b, g, pt: (0, 0))]
                 + [page_spec(r) for r in range(npg)] + [page_spec(r) for r in range(npg)],
        out_specs=pl.BlockSpec((1, SB_WIDTH, 1), lambda b, g, pt: (b, 0, 0)),
        scratch_shapes=[pltpu.VMEM((SUBLANES, 1), F32), pltpu.VMEM((SB_WIDTH, PAGE), F32)],
    )
    return pl.pallas_call(
        functools.partial(_sample_attn_kernel, npg=npg),
        grid_spec=grid_spec,
        out_shape=jax.ShapeDtypeStruct((nseq, SB_WIDTH, 1), F32),
        compiler_params=pltpu.CompilerParams(dimension_semantics=("arbitrary", "arbitrary")),
        name="sample_attention",
    )(pt_flat, q3, bias8, *([ck] * npg), *([cv] * npg))


def _sample_ffn_kernel(x_ref, yssd_ref, ysb_ref, yrg_ref, fbuf_ref, sbn_ref, wo_ref, n2_ref, wup_ref,
                       cw_ref, cb_ref, wd_ref, out_ref, fbuf_out_ref):
    ysb = _rms(ysb_ref[...], sbn_ref[...]).astype(BF16)
    mix = jnp.concatenate([yssd_ref[...], ysb, yrg_ref[...]], axis=-1)
    xm = x_ref[...] + _dot(mix, wo_ref[...])
    h2 = _rms(xm, n2_ref[...])
    gu = _dot(h2.astype(BF16), wup_ref[...])
    g = gu[:, 0:D_FF]
    conv = cb_ref[...] + fbuf_ref[0] * cw_ref[0:1, :] + fbuf_ref[1] * cw_ref[1:2, :] + g * cw_ref[2:3, :]
    fbuf_out_ref[0] = fbuf_ref[1]
    fbuf_out_ref[1] = g
    act = (_silu(conv) * gu[:, D_FF:2 * D_FF]).astype(BF16)
    out_ref[...] = xm + _dot(act, wd_ref[...])


def _sample_ffn(x, yssd, ysb, yrg, fbuf, sbn, wo, n2, wup, cw, cb, wd):
    nseq = x.shape[0]
    return pl.pallas_call(
        _sample_ffn_kernel,
        out_shape=[jax.ShapeDtypeStruct((nseq, D_MODEL), F32),
                   jax.ShapeDtypeStruct((2, nseq, D_FF), F32)],
        compiler_params=pltpu.CompilerParams(vmem_limit_bytes=VMEM_LIMIT),
        name="sample_outproj_ffn",
    )(x, yssd, ysb, yrg, fbuf, sbn, wo, n2, wup, cw, cb, wd)


def _pad_cols(a, width):
    return jnp.pad(a, ((0, 0), (0, width - a.shape[-1])))


def _block_diag(w):
    nblk, bi, bj = w.shape
    out = jnp.zeros((nblk * bi, nblk * bj), w.dtype)
    for h in range(nblk):
        out = out.at[h * bi:(h + 1) * bi, h * bj:(h + 1) * bj].set(w[h])
    return out


def kernel(x_prompt, x_sample, cache_k, cache_v, page_table, state_ssm, state_ssm_conv, state_rg, state_rg_conv, state_ffn_conv, meta_tokens, norm1, w_in, ssd_conv_w, ssd_conv_b, ssd_dt_bias, ssd_a_log, ssd_d, ssd_norm, q_norm, k_norm, sb_bias, sb_out_norm, rg_conv_w, rg_conv_b, rg_wa, rg_ba, rg_wx, rg_bx, rg_lambda, rg_out_norm, w_out, norm2, w_up, ffn_conv_w, ffn_conv_b, w_down):
    depth = w_in.shape[0]
    nb, seq, _ = x_prompt.shape
    nseq = x_sample.shape[0]
    n_pool = cache_k.shape[1]
    n_pages = page_table.shape[1]
    t_real = N_META + seq
    t_pad = -(-t_real // ATT_TQ) * ATT_TQ
    p0 = t_pad - t_real
    nk = t_pad // ATT_TK

    meta = jnp.broadcast_to(meta_tokens[None], (nb, N_META, D_MODEL))
    xp = jnp.concatenate([jnp.zeros((nb, p0, D_MODEL), F32), meta, x_prompt], axis=1)
    xs = x_sample.reshape(1, nseq, D_MODEL)

    head_of = np.arange(SB_WIDTH) // HEAD_DIM
    bd = jnp.asarray(head_of[:, None] == head_of[None, :], BF16)
    key_valid = jnp.arange(t_pad) >= p0
    ck = cache_k.transpose(0, 1, 3, 4, 2).reshape(depth * n_pool, SB_WIDTH, PAGE)
    cv = cache_v.transpose(0, 1, 3, 4, 2).reshape(depth * n_pool, SB_WIDTH, PAGE)

    outs_p, outs_s = [], []
    for l in range(depth):
        wz, wxbc, wdt, wq, wk, wv, wxr, wgr = jnp.split(
            w_in[l].T, np.cumsum([384, 896, 6, 384, 384, 384, 256])[:].tolist(), axis=0)
        wdt = jnp.pad(wdt, ((0, DT_COLS - wdt.shape[0]), (0, 0)))
        w1 = jnp.concatenate([wz, wxbc, wq, wk, wv, wxr, wgr, wdt], axis=0).astype(BF16)
        g1 = norm1[l][None]
        qn = jnp.tile(q_norm[l], SB_HEADS)[None]
        kn = jnp.tile(k_norm[l], SB_HEADS)[None]
        dtb = _pad_cols(ssd_dt_bias[l][None], DT_COLS)
        alog = _pad_cols(ssd_a_log[l][None], DT_COLS)
        dvec = jnp.repeat(ssd_d[l], HEAD_DIM)[None]
        snw = ssd_norm[l][None]
        scw, scb = ssd_conv_w[l], ssd_conv_b[l][None]
        rcw, rcb = rg_conv_w[l], rg_conv_b[l][None]
        wax = jnp.concatenate([_block_diag(rg_wa[l]), _block_diag(rg_wx[l])], axis=1)
        bax = jnp.concatenate([rg_ba[l].reshape(1, RG_WIDTH), rg_bx[l].reshape(1, RG_WIDTH)], axis=1)
        lam = rg_lambda[l][None]
        rnw = rg_out_norm[l][None]
        sbn = sb_out_norm[l][None]
        wo = w_out[l].astype(BF16)
        n2 = norm2[l][None]
        wup = w_up[l].astype(BF16)
        fcw, fcb = ffn_conv_w[l], ffn_conv_b[l][None]
        wd = w_down[l].astype(BF16)

        z, xbc, dt, qb, kb, vb, kf, vf, xr, gr = _inproj(xp, g1, w1, bd, qn, kn, tm=ROW_TILE, p0=p0)
        yssd, ssm_p, sconv_p = _ssd(xbc, z, dt, scw, scb, dtb, alog, dvec, snw, c=CHUNK, p0=p0)
        yrg, rgh_p, rconv_p = _rg(xr, gr, rcw, rcb, wax, bax, lam, rnw, c=CHUNK, p0=p0)
        bias2 = LOG2E * sb_bias[l]
        kbias = jnp.where(key_valid[None, :], bias2[:, None], NEG_BIG)
        kbias = kbias.reshape(3, 2, nk, ATT_TK).transpose(0, 2, 1, 3)
        kbias = jnp.pad(kbias, ((0, 0), (0, 0), (0, SUBLANES - 2), (0, 0)))
        ysb = _attn(qb, kb, vb, kbias, tq=ATT_TQ, tk=ATT_TK)
        head_rows = p0 + N_META
        skip_rows = head_rows if (l == depth - 1 and head_rows % ROW_TILE == 0) else 0
        xp, fconv_p = _ffn(xp, yssd, ysb, yrg, sbn, wo, n2, wup, fcw, fcb, wd, tm=ROW_TILE, p0=p0,
                           skip_rows=skip_rows)
        outs_p.append((kf[:, p0:].reshape(nb, t_real, SB_HEADS, HEAD_DIM),
                       vf[:, p0:].reshape(nb, t_real, SB_HEADS, HEAD_DIM),
                       ssm_p.reshape(nb, SSD_HEADS, HEAD_DIM, SSD_STATE),
                       sconv_p[:, SUBLANES - 3:],
                       rgh_p.reshape(nb, RG_WIDTH),
                       rconv_p[:, SUBLANES - 3:],
                       fconv_p[:, SUBLANES - 2:]))

        z, xbc, dt, qb, kb, vb, kf, vf, xr, gr = _inproj(xs, g1, w1, bd, qn, kn, tm=nseq, p0=0)
        yssd, ssm_s, sconv_s, yrg, rgh_s, rconv_s = _sample_mix(
            z[0], xbc[0], dt[0], state_ssm_conv[l].transpose(1, 0, 2),
            state_ssm[l].reshape(nseq, SSD_WIDTH, SSD_STATE), xr[0], gr[0],
            state_rg_conv[l].transpose(1, 0, 2), state_rg[l],
            scw, scb, dtb, alog, dvec, snw, rcw, rcb, wax, bax, lam, rnw)
        bias8 = jnp.broadcast_to(jnp.pad(bias2, (0, SUBLANES - SB_HEADS))[:, None], (SUBLANES, PAGE))
        pt_flat = (page_table + l * n_pool).reshape(-1).astype(jnp.int32)
        q_lanes = jnp.broadcast_to(qb[0].astype(F32)[:, :, None], (nseq, SB_WIDTH, PAGE))
        ysb = _sample_attn(pt_flat, q_lanes, bias8, ck, cv,
                           nseq=nseq, n_pages=n_pages, npg=PAGES_PER_STEP)
        xs2, fconv_s = _sample_ffn(xs[0], yssd, ysb.reshape(nseq, SB_WIDTH), yrg,
                                   state_ffn_conv[l].transpose(1, 0, 2), sbn, wo, n2, wup, fcw, fcb, wd)
        xs = xs2[None]
        outs_s.append((kf.reshape(nseq, 1, SB_HEADS, HEAD_DIM),
                       vf.reshape(nseq, 1, SB_HEADS, HEAD_DIM),
                       ssm_s.reshape(nseq, SSD_HEADS, HEAD_DIM, SSD_STATE),
                       sconv_s.transpose(1, 0, 2),
                       rgh_s,
                       rconv_s.transpose(1, 0, 2),
                       fconv_s.transpose(1, 0, 2)))

    def stk(outs, i):
        return jnp.stack([o[i] for o in outs], axis=0)

    y_prompt = xp[:, p0 + N_META - skip_rows:]
    y_sample = xs.reshape(nseq, 1, D_MODEL)
    return (y_prompt, y_sample,
            stk(outs_p, 0), stk(outs_p, 1), stk(outs_p, 2), stk(outs_p, 3), stk(outs_p, 4), stk(outs_p, 5), stk(outs_p, 6),
            stk(outs_s, 0), stk(outs_s, 1), stk(outs_s, 2), stk(outs_s, 3), stk(outs_s, 4), stk(outs_s, 5), stk(outs_s, 6))
```
